```python
import jax, jax.numpy as jnp
from jax import lax
import numpy as np

D_MODEL = 1024
BATCH = 8
SEQ = 2048
DEPTH = 1

ATTN_HEADS = 8
KV_HEADS = 2
HEAD_DIM = 64
ATTN_WIDTH = ATTN_HEADS * HEAD_DIM
KV_WIDTH = KV_HEADS * HEAD_DIM
CONV_CHANNELS = D_MODEL - ATTN_WIDTH
MIX_WIDTH = ATTN_WIDTH + CONV_CHANNELS
IN_WIDTH = ATTN_WIDTH + 2 * KV_WIDTH + 2 * CONV_CHANNELS
CONV_KERNEL = 31
WINDOW = 128
BLOCK = 128
ROPE_THETA = 10000.0
PEER_HEADS = 8
N_KEYS = 128
N_EXPERTS = N_KEYS * N_KEYS
PEER_QDIM = 256
PEER_TOPK = 16
PEER_BLOCK = 128
EPS = 1e-6
NEG = -1e30

kernel_name = "hymba_conformer_swa_peer_block"


def rms_norm(x, g):
    xf = x.astype(jnp.float32)
    y = xf * lax.rsqrt(jnp.mean(xf * xf, axis=-1, keepdims=True) + EPS)
    return (y * g.astype(jnp.float32)).astype(x.dtype)


def layer_norm(x, g, b):
    xf = x.astype(jnp.float32)
    mu = jnp.mean(xf, axis=-1, keepdims=True)
    xc = xf - mu
    y = xc * lax.rsqrt(jnp.mean(xc * xc, axis=-1, keepdims=True) + EPS)
    return (y * g.astype(jnp.float32) + b.astype(jnp.float32)).astype(x.dtype)


def rope(x, pos):
    half = x.shape[-1] // 2
    inv = ROPE_THETA ** (-jnp.arange(half, dtype=jnp.float32) / half)
    ang = pos.astype(jnp.float32)[:, None] * inv[None, :]
    cos = jnp.cos(ang)[None, :, None, :]
    sin = jnp.sin(ang)[None, :, None, :]
    xf = x.astype(jnp.float32)
    x1, x2 = xf[..., :half], xf[..., half:]
    return jnp.concatenate([x1 * cos - x2 * sin, x2 * cos + x1 * sin], axis=-1).astype(x.dtype)


def sliding_window_attention(q, k, v, sinks):
    B, S, H, d = q.shape
    nb = S // BLOCK
    G = H // KV_HEADS
    qb = q.reshape(B, nb, BLOCK, KV_HEADS, G, d)
    kb = k.reshape(B, nb, BLOCK, KV_HEADS, d)
    vb = v.reshape(B, nb, BLOCK, KV_HEADS, d)
    pad = ((0, 0), (1, 0), (0, 0), (0, 0), (0, 0))
    kcat = jnp.concatenate([jnp.pad(kb, pad)[:, :-1], kb], axis=2)
    vcat = jnp.concatenate([jnp.pad(vb, pad)[:, :-1], vb], axis=2)
    s = jnp.einsum('bnqhgd,bnkhd->bnhgqk', qb, kcat).astype(jnp.float32) * (d ** -0.5)
    qi = jnp.arange(BLOCK)[:, None]
    kj = jnp.arange(2 * BLOCK)[None, :]
    rel = qi + BLOCK - kj
    band = (rel >= 0) & (rel < WINDOW)
    exists = (jnp.arange(nb)[:, None, None] > 0) | (kj[None] >= BLOCK)
    mask = band[None] & exists
    s = jnp.where(mask[None, :, None, None], s, NEG)
    sink = sinks.astype(jnp.float32).reshape(KV_HEADS, G)[None, None, :, :, None, None]
    m = jnp.maximum(jnp.max(s, axis=-1, keepdims=True), sink)
    p = jnp.exp(s - m)
    p = p / (jnp.sum(p, axis=-1, keepdims=True) + jnp.exp(sink - m))
    o = jnp.einsum('bnhgqk,bnkhd->bnqhgd', p.astype(v.dtype), vcat)
    return o.reshape(B, S, H * d)


def conformer_conv(a, b, conv_w, conv_b, ln_g, ln_b):
    C = a.shape[-1]
    u = a * jax.nn.sigmoid(b)
    up = jnp.pad(u, ((0, 0), (CONV_KERNEL - 1, 0), (0, 0)))
    y = lax.conv_general_dilated(up, conv_w[:, None, :].astype(u.dtype), window_strides=(1,),
                                 padding='VALID', dimension_numbers=('NWC', 'WIO', 'NWC'),
                                 feature_group_count=C) + conv_b
    return jax.nn.silu(layer_norm(y, ln_g, ln_b))


def peer_ffn(x, w_q, sub_keys, u_tab, v_tab):
    B, S, D = x.shape
    xt = x.reshape(-1, PEER_BLOCK, D)

    def block(xb):
        T = xb.shape[0]
        q = (xb @ w_q).reshape(T, PEER_HEADS, 2, PEER_QDIM // 2)
        s = jnp.einsum('thpc,hpkc->thpk', q, sub_keys).astype(jnp.float32)
        top_s, top_i = lax.top_k(s, PEER_TOPK)
        cand = top_s[:, :, 0, :, None] + top_s[:, :, 1, None, :]
        best_s, best_c = lax.top_k(cand.reshape(T, PEER_HEADS, PEER_TOPK * PEER_TOPK), PEER_TOPK)
        i1 = jnp.take_along_axis(top_i[:, :, 0], best_c // PEER_TOPK, axis=-1)
        i2 = jnp.take_along_axis(top_i[:, :, 1], best_c % PEER_TOPK, axis=-1)
        ids = i1 * N_KEYS + i2
        g = jax.nn.softmax(best_s, axis=-1)
        ue = u_tab[ids]
        ve = v_tab[ids]
        act = jax.nn.gelu(jnp.einsum('td,thkd->thk', xb, ue).astype(jnp.float32), approximate=False)
        w = (g * act).astype(xb.dtype)
        return jnp.einsum('thk,thkd->td', w, ve)

    return lax.map(block, xt).reshape(B, S, D)


def setup_inputs(seed: int = 0) -> dict:
    key = jax.random.key(seed)
    ks = jax.random.split(key, 20)
    f32 = jnp.float32
    L = DEPTH

    def gain(k, n):
        return (1.0 + 0.02 * jax.random.normal(k, (L, n), f32))

    return {
        "x": jax.random.normal(ks[0], (BATCH, SEQ, D_MODEL), f32),
        "norm_mix_g": gain(ks[1], D_MODEL),
        "w_in": jax.random.normal(ks[2], (L, D_MODEL, IN_WIDTH), f32) * D_MODEL ** -0.5,
        "conv_w": jax.random.normal(ks[3], (L, CONV_KERNEL, CONV_CHANNELS), f32) * CONV_KERNEL ** -0.5,
        "conv_b": 0.02 * jax.random.normal(ks[4], (L, CONV_CHANNELS), f32),
        "conv_ln_g": gain(ks[5], CONV_CHANNELS),
        "conv_ln_b": 0.02 * jax.random.normal(ks[6], (L, CONV_CHANNELS), f32),
        "attn_sinks": 0.5 * jax.random.normal(ks[7], (L, ATTN_HEADS), f32),
        "attn_out_g": gain(ks[8], ATTN_WIDTH),
        "conv_out_g": gain(ks[9], CONV_CHANNELS),
        "w_out": jax.random.normal(ks[10], (L, MIX_WIDTH, D_MODEL), f32) * (0.5 * MIX_WIDTH ** -0.5),
        "norm_ffn_g": gain(ks[11], D_MODEL),
        "peer_w_q": jax.random.normal(ks[12], (L, D_MODEL, PEER_HEADS * PEER_QDIM), f32) * D_MODEL ** -0.5,
        "peer_sub_keys": jax.random.normal(ks[13], (L, PEER_HEADS, 2, N_KEYS, PEER_QDIM // 2), f32) * (PEER_QDIM // 2) ** -0.5,
        "peer_u": jax.random.normal(ks[14], (L, N_EXPERTS, D_MODEL), f32) * D_MODEL ** -0.5,
        "peer_v": jax.random.normal(ks[15], (L, N_EXPERTS, D_MODEL), f32) * 0.5,
        "final_norm_g": (1.0 + 0.02 * jax.random.normal(ks[16], (D_MODEL,), f32)),
    }


def reference(x, norm_mix_g, w_in, conv_w, conv_b, conv_ln_g, conv_ln_b, attn_sinks,
              attn_out_g, conv_out_g, w_out, norm_ffn_g, peer_w_q, peer_sub_keys,
              peer_u, peer_v, final_norm_g):
    B, S, _ = x.shape
    pos = jnp.arange(S, dtype=jnp.int32)
    splits = [ATTN_WIDTH, ATTN_WIDTH + KV_WIDTH, ATTN_WIDTH + 2 * KV_WIDTH,
              ATTN_WIDTH + 2 * KV_WIDTH + CONV_CHANNELS]
    h = x
    for l in range(DEPTH):
        hn = rms_norm(h, norm_mix_g[l])
        proj = hn @ w_in[l]
        q, k, v, ga, gb = jnp.split(proj, splits, axis=-1)
        q = rope(q.reshape(B, S, ATTN_HEADS, HEAD_DIM), pos)
        k = rope(k.reshape(B, S, KV_HEADS, HEAD_DIM), pos)
        v = v.reshape(B, S, KV_HEADS, HEAD_DIM)
        attn = sliding_window_attention(q, k, v, attn_sinks[l])
        conv = conformer_conv(ga, gb, conv_w[l], conv_b[l], conv_ln_g[l], conv_ln_b[l])
        mixed = jnp.concatenate([rms_norm(attn, attn_out_g[l]), rms_norm(conv, conv_out_g[l])], axis=-1)
        h = h + mixed @ w_out[l]
        hn = rms_norm(h, norm_ffn_g[l])
        h = h + peer_ffn(hn, peer_w_q[l], peer_sub_keys[l], peer_u[l], peer_v[l])
    return rms_norm(h, final_norm_g)
```

```python
import functools

import jax
import jax.numpy as jnp
from jax import lax
from jax.experimental import pallas as pl
from jax.experimental.pallas import tpu as pltpu

F32 = jnp.float32
BF16 = jnp.bfloat16

D_MODEL = 1024
ATTN_HEADS = 8
KV_HEADS = 2
HEAD_DIM = 64
ATTN_WIDTH = ATTN_HEADS * HEAD_DIM
KV_WIDTH = KV_HEADS * HEAD_DIM
CONV_CHANNELS = D_MODEL - ATTN_WIDTH
IN_WIDTH = ATTN_WIDTH + 2 * KV_WIDTH + 2 * CONV_CHANNELS
CONV_KERNEL = 31
WINDOW = 128
ROPE_THETA = 10000.0
PEER_HEADS = 8
N_KEYS = 128
N_EXPERTS = N_KEYS * N_KEYS
PEER_QDIM = 256
PEER_TOPK = 16
EPS = 1e-6
NEG = -1e30

LANES = 128
SUBLANES = 8
VMEM_LIMIT = 56 * 1024 * 1024

TM_PROJ = 512
TS_CONV = 256
CONV_HALO = 32
CONV_SUB = 64
TB_ROUTE = 128
TB_PEER = 256
EC_PEER = 2048
ESUB_PEER = 512
GATE_PITCH = TB_PEER + SUBLANES


def _rms(x, g):
    return x * lax.rsqrt(jnp.mean(x * x, axis=-1, keepdims=True) + EPS) * g


def _cparams(sem):
    return pltpu.CompilerParams(dimension_semantics=sem, vmem_limit_bytes=VMEM_LIMIT)


def _inproj_kernel(x_ref, g_ref, w_ref, cos_ref, sin_ref, q_ref, k_ref, v_ref, u_ref):
    hn = _rms(x_ref[...], g_ref[...]).astype(BF16)
    proj = jnp.dot(hn, w_ref[...], preferred_element_type=F32)
    cos = cos_ref[...]
    sin = sin_ref[...]
    lane = lax.broadcasted_iota(jnp.int32, cos.shape, 1)
    first_half = (lane % HEAD_DIM) < (HEAD_DIM // 2)

    def rope(z):
        partner = jnp.where(first_half,
                            pltpu.roll(z, LANES - HEAD_DIM // 2, 1),
                            pltpu.roll(z, HEAD_DIM // 2, 1))
        return z * cos + partner * sin

    scale = HEAD_DIM ** -0.5
    for c in range(ATTN_WIDTH // LANES):
        sl = slice(c * LANES, (c + 1) * LANES)
        q_ref[:, sl] = (rope(proj[:, sl]) * scale).astype(BF16)
    k_ref[...] = rope(proj[:, ATTN_WIDTH:ATTN_WIDTH + KV_WIDTH]).astype(BF16)
    v_ref[...] = proj[:, ATTN_WIDTH + KV_WIDTH:ATTN_WIDTH + 2 * KV_WIDTH].astype(BF16)
    c0 = ATTN_WIDTH + 2 * KV_WIDTH
    ga = proj[:, c0:c0 + CONV_CHANNELS]
    gb = proj[:, c0 + CONV_CHANNELS:c0 + 2 * CONV_CHANNELS]
    u_ref[...] = ga * jax.nn.sigmoid(gb)


def _inproj(x2, g, w_bf, cos_t, sin_t, seq):
    n = x2.shape[0]
    tm = TM_PROJ
    sb = seq // tm
    return pl.pallas_call(
        _inproj_kernel,
        grid=(n // tm,),
        in_specs=[
            pl.BlockSpec((tm, D_MODEL), lambda i: (i, 0)),
            pl.BlockSpec((1, D_MODEL), lambda i: (0, 0)),
            pl.BlockSpec((D_MODEL, IN_WIDTH), lambda i: (0, 0)),
            pl.BlockSpec((tm, LANES), lambda i: (i % sb, 0)),
            pl.BlockSpec((tm, LANES), lambda i: (i % sb, 0)),
        ],
        out_specs=[
            pl.BlockSpec((tm, ATTN_WIDTH), lambda i: (i, 0)),
            pl.BlockSpec((tm, KV_WIDTH), lambda i: (i, 0)),
            pl.BlockSpec((tm, KV_WIDTH), lambda i: (i, 0)),
            pl.BlockSpec((tm, CONV_CHANNELS), lambda i: (i, 0)),
        ],
        out_shape=[
            jax.ShapeDtypeStruct((n, ATTN_WIDTH), BF16),
            jax.ShapeDtypeStruct((n, KV_WIDTH), BF16),
            jax.ShapeDtypeStruct((n, KV_WIDTH), BF16),
            jax.ShapeDtypeStruct((n, CONV_CHANNELS), F32),
        ],
        compiler_params=_cparams(("arbitrary",)),
        name="inproj",
    )(x2, g, w_bf, cos_t, sin_t)


def _attn_kernel(sink_ref, q_ref, kp_ref, kc_ref, vp_ref, vc_ref, g_ref, o_ref):
    n = pl.program_id(1)
    q = q_ref[...]
    kcat = jnp.concatenate([kp_ref[...], kc_ref[...]], axis=0)
    vcat = jnp.concatenate([vp_ref[...], vc_ref[...]], axis=0)
    qi = lax.broadcasted_iota(jnp.int32, (WINDOW, 2 * WINDOW), 0)
    kj = lax.broadcasted_iota(jnp.int32, (WINDOW, 2 * WINDOW), 1)
    rel = qi + WINDOW - kj
    mask = (rel >= 0) & (rel < WINDOW) & ((kj >= WINDOW) | (n > 0))
    group = ATTN_HEADS // KV_HEADS
    outs = []
    for h in range(ATTN_HEADS):
        kv = h // group
        qh = q[:, h * HEAD_DIM:(h + 1) * HEAD_DIM]
        kg = kcat[:, kv * HEAD_DIM:(kv + 1) * HEAD_DIM]
        vg = vcat[:, kv * HEAD_DIM:(kv + 1) * HEAD_DIM]
        s = lax.dot_general(qh, kg, (((1,), (1,)), ((), ())), preferred_element_type=F32)
        s = jnp.where(mask, s, NEG)
        sink = sink_ref[h]
        m = jnp.maximum(jnp.max(s, axis=-1, keepdims=True), sink)
        p = jnp.exp(s - m)
        p = p / (jnp.sum(p, axis=-1, keepdims=True) + jnp.exp(sink - m))
        outs.append(jnp.dot(p.astype(BF16), vg, preferred_element_type=F32))
    attn = jnp.concatenate(outs, axis=1)
    o_ref[...] = _rms(attn, g_ref[...]).astype(BF16)


def _attention(sinks, q, k, v, g, batch, seq):
    n = q.shape[0]
    nb = seq // WINDOW
    cur = lambda b, i: (b * nb + i, 0)
    prev = lambda b, i: (b * nb + jnp.maximum(i - 1, 0), 0)
    return pl.pallas_call(
        _attn_kernel,
        grid=(batch, nb),
        in_specs=[
            pl.BlockSpec(memory_space=pltpu.SMEM),
            pl.BlockSpec((WINDOW, ATTN_WIDTH), cur),
            pl.BlockSpec((WINDOW, KV_WIDTH), prev),
            pl.BlockSpec((WINDOW, KV_WIDTH), cur),
            pl.BlockSpec((WINDOW, KV_WIDTH), prev),
            pl.BlockSpec((WINDOW, KV_WIDTH), cur),
            pl.BlockSpec((1, ATTN_WIDTH), lambda b, i: (0, 0)),
        ],
        out_specs=pl.BlockSpec((WINDOW, ATTN_WIDTH), cur),
        out_shape=jax.ShapeDtypeStruct((n, ATTN_WIDTH), BF16),
        compiler_params=_cparams(("arbitrary", "arbitrary")),
        name="swa",
    )(sinks, q, k, k, v, v, g)


def _conv_kernel(up_ref, uc_ref, w_ref, cb_ref, lg_ref, lb_ref, og_ref, o_ref, buf_ref):
    n = pl.program_id(1)
    ts = uc_ref.shape[0]
    tail = up_ref[ts - CONV_HALO:ts, :]
    buf_ref[0:CONV_HALO, :] = jnp.where(n > 0, tail, 0.0)
    buf_ref[CONV_HALO:CONV_HALO + ts, :] = uc_ref[...]
    off = CONV_HALO - (CONV_KERNEL - 1)
    for r0 in range(0, ts, CONV_SUB):
        acc = jnp.zeros((CONV_SUB, CONV_CHANNELS), F32)
        for k in range(CONV_KERNEL):
            acc = acc + w_ref[k:k + 1, :] * buf_ref[r0 + off + k:r0 + off + k + CONV_SUB, :]
        y = acc + cb_ref[...]
        mu = jnp.mean(y, axis=-1, keepdims=True)
        yc = y - mu
        z = yc * lax.rsqrt(jnp.mean(yc * yc, axis=-1, keepdims=True) + EPS) * lg_ref[...] + lb_ref[...]
        c = z * jax.nn.sigmoid(z)
        o_ref[r0:r0 + CONV_SUB, :] = _rms(c, og_ref[...]).astype(BF16)


def _conv(u, w, cb, lg, lb, og, batch, seq):
    n = u.shape[0]
    ts = TS_CONV
    nb = seq // ts
    cur = lambda b, i: (b * nb + i, 0)
    prev = lambda b, i: (b * nb + jnp.maximum(i - 1, 0), 0)
    vec = pl.BlockSpec((1, CONV_CHANNELS), lambda b, i: (0, 0))
    return pl.pallas_call(
        _conv_kernel,
        grid=(batch, nb),
        in_specs=[
            pl.BlockSpec((ts, CONV_CHANNELS), prev),
            pl.BlockSpec((ts, CONV_CHANNELS), cur),
            pl.BlockSpec((CONV_KERNEL, CONV_CHANNELS), lambda b, i: (0, 0)),
            vec, vec, vec, vec,
        ],
        out_specs=pl.BlockSpec((ts, CONV_CHANNELS), cur),
        out_shape=jax.ShapeDtypeStruct((n, CONV_CHANNELS), BF16),
        scratch_shapes=[pltpu.VMEM((CONV_HALO + ts, CONV_CHANNELS), F32)],
        compiler_params=_cparams(("arbitrary", "arbitrary")),
        name="conv",
    )(u, u, w, cb, lg, lb, og)


def _outproj_kernel(x_ref, ma_ref, mc_ref, wo_ref, g_ref, wq_ref, h_ref, hn_ref, pq_ref):
    h = (x_ref[...]
         + jnp.dot(ma_ref[...], wo_ref[0:ATTN_WIDTH, :], preferred_element_type=F32)
         + jnp.dot(mc_ref[...], wo_ref[ATTN_WIDTH:D_MODEL, :], preferred_element_type=F32))
    h_ref[...] = h
    hn = _rms(h, g_ref[...]).astype(BF16)
    hn_ref[...] = hn
    pq_ref[...] = jnp.dot(hn, wq_ref[...], preferred_element_type=F32).astype(BF16)


def _outproj(x2, ma, mc, wo_bf, g, wq_bf):
    n = x2.shape[0]
    tm = TM_PROJ
    qw = PEER_HEADS * PEER_QDIM
    row = lambda i: (i, 0)
    fixed = lambda i: (0, 0)
    return pl.pallas_call(
        _outproj_kernel,
        grid=(n // tm,),
        in_specs=[
            pl.BlockSpec((tm, D_MODEL), row),
            pl.BlockSpec((tm, ATTN_WIDTH), row),
            pl.BlockSpec((tm, CONV_CHANNELS), row),
            pl.BlockSpec((D_MODEL, D_MODEL), fixed),
            pl.BlockSpec((1, D_MODEL), fixed),
            pl.BlockSpec((D_MODEL, qw), fixed),
        ],
        out_specs=[
            pl.BlockSpec((tm, D_MODEL), row),
            pl.BlockSpec((tm, D_MODEL), row),
            pl.BlockSpec((tm, qw), row),
        ],
        out_shape=[
            jax.ShapeDtypeStruct((n, D_MODEL), F32),
            jax.ShapeDtypeStruct((n, D_MODEL), BF16),
            jax.ShapeDtypeStruct((n, qw), BF16),
        ],
        compiler_params=_cparams(("arbitrary",)),
        name="outproj",
    )(x2, ma, mc, wo_bf, g, wq_bf)


def _topk_rows(s, row_id, count, payload=None):
    big = jnp.int32(2 ** 30)
    vals, ids, pays = [], [], []
    for _ in range(count):
        m = jnp.max(s, axis=0, keepdims=True)
        win = jnp.min(jnp.where(s == m, row_id, big), axis=0, keepdims=True)
        hit = row_id == win
        if payload is not None:
            pays.append(jnp.max(jnp.where(hit, payload, -1), axis=0, keepdims=True))
        s = jnp.where(hit, -jnp.inf, s)
        vals.append(m)
        ids.append(win)
    return vals, ids, pays


def _route_kernel(pq_ref, keys_ref, ids_ref, gate_ref, ids_scr, gate_scr):
    tb = pq_ref.shape[0]
    key_row = lax.broadcasted_iota(jnp.int32, (N_KEYS, tb), 0)
    r8 = lax.broadcasted_iota(jnp.int32, (SUBLANES, tb), 0)
    half = PEER_QDIM // 2

    def head(h, carry):
        tops = []
        for p in range(2):
            hp = h * 2 + p
            qhp = pq_ref[:, pl.ds(pl.multiple_of(hp * half, half), half)]
            s = lax.dot_general(keys_ref[hp], qhp, (((1,), (1,)), ((), ())),
                                preferred_element_type=F32)
            vals, idx, _ = _topk_rows(s, key_row, PEER_TOPK)
            tops.append((vals, idx))
        (va, ia), (vb, ib) = tops
        lo = SUBLANES
        vb_lo = jnp.concatenate(vb[:lo], axis=0)
        ib_lo = jnp.concatenate(ib[:lo], axis=0)
        cand, cid, eid = [], [], []
        for k1 in range(lo):
            ok = (k1 + 1) * (r8 + 1) <= PEER_TOPK
            cand.append(jnp.where(ok, va[k1] + vb_lo, -jnp.inf))
            cid.append(k1 * PEER_TOPK + r8)
            eid.append(ia[k1] * N_KEYS + ib_lo)
        cand.append(va[0] + jnp.concatenate(vb[lo:], axis=0))
        cid.append(lo + r8)
        eid.append(ia[0] * N_KEYS + jnp.concatenate(ib[lo:], axis=0))
        cand.append(jnp.concatenate(va[lo:], axis=0) + vb[0])
        cid.append((lo + r8) * PEER_TOPK)
        eid.append(jnp.concatenate(ia[lo:], axis=0) * N_KEYS + ib[0])
        cand = jnp.concatenate(cand, axis=0)
        cid = jnp.concatenate(cid, axis=0)
        eid = jnp.concatenate(eid, axis=0)
        best, _, experts = _topk_rows(cand, cid, PEER_TOPK, payload=eid)
        e = [jnp.exp(b - best[0]) for b in best]
        tot = e[0]
        for x in e[1:]:
            tot = tot + x
        row0 = pl.multiple_of(h * PEER_TOPK, PEER_TOPK)
        gate_scr[pl.ds(row0, PEER_TOPK), :] = jnp.concatenate([x / tot for x in e], axis=0)
        ids_scr[pl.ds(row0, PEER_TOPK), :] = jnp.concatenate(experts, axis=0)
        return carry

    lax.fori_loop(0, PEER_HEADS, head, 0)
    ids_ref[...] = ids_scr[...].T
    gate_ref[...] = gate_scr[...].T


def _route(pq, keys_bf):
    n = pq.shape[0]
    tb = TB_ROUTE
    nsel = PEER_HEADS * PEER_TOPK
    return pl.pallas_call(
        _route_kernel,
        grid=(n // tb,),
        in_specs=[
            pl.BlockSpec((tb, PEER_HEADS * PEER_QDIM), lambda i: (i, 0)),
            pl.BlockSpec((2 * PEER_HEADS, N_KEYS, PEER_QDIM // 2), lambda i: (0, 0, 0)),
        ],
        out_specs=[
            pl.BlockSpec((tb, nsel), lambda i: (i, 0)),
            pl.BlockSpec((tb, nsel), lambda i: (i, 0)),
        ],
        out_shape=[
            jax.ShapeDtypeStruct((n, nsel), jnp.int32),
            jax.ShapeDtypeStruct((n, nsel), F32),
        ],
        scratch_shapes=[pltpu.VMEM((nsel, tb), jnp.int32), pltpu.VMEM((nsel, tb), F32)],
        compiler_params=_cparams(("arbitrary",)),
        name="peer_route",
    )(pq, keys_bf)


def _peer_kernel(ids_ref, gate_ref, hn_ref, h_ref, ut_ref, v_ref, fg_ref, o_ref, gs_ref, w_ref, acc_ref, *,
                 final_norm):
    c = pl.program_id(1)
    tb = hn_ref.shape[0]

    @pl.when(c == 0)
    def _scatter():
        sub = lax.broadcasted_iota(jnp.int32, (N_KEYS, ids_ref.shape[1]), 0)

        def token(t, carry):
            ids = ids_ref[pl.ds(t, 1), :]
            gate = gate_ref[pl.ds(t, 1), :]
            k1 = lax.shift_right_logical(ids, 7)
            k2 = ids & (N_KEYS - 1)
            m1 = jnp.where(sub == k1, gate, 0.0).astype(BF16)
            m2 = jnp.where(sub == k2, 1.0, 0.0).astype(BF16)
            tile = lax.dot_general(m1, m2, (((1,), (1,)), ((), ())), preferred_element_type=F32)
            gs_ref[pl.ds(t, N_KEYS, stride=GATE_PITCH), :] = tile
            return carry

        lax.fori_loop(0, tb, token, 0)
        acc_ref[...] = jnp.zeros_like(acc_ref)

    hn = hn_ref[...]
    keys_per_sub = ESUB_PEER // N_KEYS
    for sc in range(EC_PEER // ESUB_PEER):
        sl = slice(sc * ESUB_PEER, (sc + 1) * ESUB_PEER)
        act = jnp.dot(hn, ut_ref[:, sl], preferred_element_type=F32)
        k1_0 = c * (EC_PEER // N_KEYS) + sc * keys_per_sub
        gates = jnp.concatenate(
            [gs_ref[pl.ds(pl.multiple_of((k1_0 + i) * GATE_PITCH, SUBLANES), tb), :] for i in range(keys_per_sub)],
            axis=1)
        gelu = 0.5 * act * (1.0 + lax.erf(act * (2.0 ** -0.5)))
        w_ref[:, sl] = (gelu * gates).astype(BF16)
    acc_ref[...] += jnp.dot(w_ref[...], v_ref[...], preferred_element_type=F32)

    @pl.when(c == pl.num_programs(1) - 1)
    def _finish():
        out = h_ref[...] + acc_ref[...]
        o_ref[...] = _rms(out, fg_ref[...]) if final_norm else out


def _peer(ids, gate, hn, h, ut_bf, v_bf, fg, final_norm):
    n = hn.shape[0]
    tb = TB_PEER
    nsel = PEER_HEADS * PEER_TOPK
    tok = lambda i, c: (i, 0)
    return pl.pallas_call(
        functools.partial(_peer_kernel, final_norm=final_norm),
        grid=(n // tb, N_EXPERTS // EC_PEER),
        in_specs=[
            pl.BlockSpec((tb, nsel), tok),
            pl.BlockSpec((tb, nsel), tok),
            pl.BlockSpec((tb, D_MODEL), tok),
            pl.BlockSpec((tb, D_MODEL), tok),
            pl.BlockSpec((D_MODEL, EC_PEER), lambda i, c: (0, c)),
            pl.BlockSpec((EC_PEER, D_MODEL), lambda i, c: (c, 0)),
            pl.BlockSpec((1, D_MODEL), lambda i, c: (0, 0)),
        ],
        out_specs=pl.BlockSpec((tb, D_MODEL), tok),
        out_shape=jax.ShapeDtypeStruct((n, D_MODEL), F32),
        scratch_shapes=[
            pltpu.VMEM((N_KEYS * GATE_PITCH, N_KEYS), F32),
            pltpu.VMEM((tb, EC_PEER), BF16),
            pltpu.VMEM((tb, D_MODEL), F32),
        ],
        compiler_params=_cparams(("arbitrary", "arbitrary")),
        name="peer_experts",
    )(ids, gate, hn, h, ut_bf, v_bf, fg)


def _rope_tables(seq):
    half = HEAD_DIM // 2
    inv = ROPE_THETA ** (-jnp.arange(half, dtype=F32) / half)
    ang = jnp.arange(seq, dtype=jnp.int32).astype(F32)[:, None] * inv[None, :]
    cos = jnp.cos(ang)
    sin = jnp.sin(ang)
    reps = LANES // HEAD_DIM
    cos_t = jnp.tile(jnp.concatenate([cos, cos], axis=1), (1, reps))
    sin_t = jnp.tile(jnp.concatenate([-sin, sin], axis=1), (1, reps))
    return cos_t, sin_t


def kernel(x, norm_mix_g, w_in, conv_w, conv_b, conv_ln_g, conv_ln_b, attn_sinks, attn_out_g, conv_out_g, w_out,
           norm_ffn_g, peer_w_q, peer_sub_keys, peer_u, peer_v, final_norm_g):
    batch, seq, _ = x.shape
    depth = w_in.shape[0]
    cos_t, sin_t = _rope_tables(seq)
    h = x.reshape(batch * seq, D_MODEL)
    row = lambda a: a.reshape(1, -1)
    for l in range(depth):
        q, k, v, u = _inproj(h, row(norm_mix_g[l]), w_in[l].astype(BF16), cos_t, sin_t, seq)
        ma = _attention(attn_sinks[l], q, k, v, row(attn_out_g[l]), batch, seq)
        mc = _conv(u, conv_w[l], row(conv_b[l]), row(conv_ln_g[l]), row(conv_ln_b[l]), row(conv_out_g[l]),
                   batch, seq)
        h, hn, pq = _outproj(h, ma, mc, w_out[l].astype(BF16), row(norm_ffn_g[l]), peer_w_q[l].astype(BF16))
        keys = peer_sub_keys[l].reshape(2 * PEER_HEADS, N_KEYS, PEER_QDIM // 2).astype(BF16)
        ids, gate = _route(pq, keys)
        h = _peer(ids, gate, hn, h, peer_u[l].T.astype(BF16), peer_v[l].astype(BF16), row(final_norm_g),
                  final_norm=(l == depth - 1))
    return h.reshape(batch, seq, D_MODEL)
```

```python
import functools

import jax
import jax.numpy as jnp
from jax import lax
from jax.experimental import pallas as pl
from jax.experimental.pallas import tpu as pltpu

F32 = jnp.float32
BF16 = jnp.bfloat16

D_MODEL = 1024
ATTN_HEADS = 8
KV_HEADS = 2
HEAD_DIM = 64
ATTN_WIDTH = ATTN_HEADS * HEAD_DIM
KV_WIDTH = KV_HEADS * HEAD_DIM
CONV_CHANNELS = D_MODEL - ATTN_WIDTH
IN_WIDTH = ATTN_WIDTH + 2 * KV_WIDTH + 2 * CONV_CHANNELS
CONV_KERNEL = 31
WINDOW = 128
ROPE_THETA = 10000.0
PEER_HEADS = 8
N_KEYS = 128
N_EXPERTS = N_KEYS * N_KEYS
PEER_QDIM = 256
PEER_TOPK = 16
EPS = 1e-6
NEG = -1e30

LANES = 128
SUBLANES = 8
VMEM_LIMIT = 56 * 1024 * 1024

TM_PROJ = 512
TS_CONV = 256
CONV_HALO = 32
CONV_SUB = 64
TB_ROUTE = 256
TB_PEER = 256
EC_PEER = 2048
ESUB_PEER = 512
SCATTER_UNROLL = 32
GATE_PITCH = TB_PEER + SUBLANES


def _rms(x, g):
    return x * lax.rsqrt(jnp.mean(x * x, axis=-1, keepdims=True) + EPS) * g


def _cparams(sem):
    return pltpu.CompilerParams(dimension_semantics=sem, vmem_limit_bytes=VMEM_LIMIT)


def _inproj_kernel(x_ref, g_ref, w_ref, cos_ref, sin_ref, q_ref, k_ref, v_ref, u_ref):
    hn = _rms(x_ref[...], g_ref[...]).astype(BF16)
    proj = jnp.dot(hn, w_ref[...], preferred_element_type=F32)
    cos = cos_ref[...]
    sin = sin_ref[...]
    lane = lax.broadcasted_iota(jnp.int32, cos.shape, 1)
    first_half = (lane % HEAD_DIM) < (HEAD_DIM // 2)

    def rope(z):
        partner = jnp.where(first_half,
                            pltpu.roll(z, LANES - HEAD_DIM // 2, 1),
                            pltpu.roll(z, HEAD_DIM // 2, 1))
        return z * cos + partner * sin

    scale = HEAD_DIM ** -0.5
    for c in range(ATTN_WIDTH // LANES):
        sl = slice(c * LANES, (c + 1) * LANES)
        q_ref[:, sl] = (rope(proj[:, sl]) * scale).astype(BF16)
    k_ref[...] = rope(proj[:, ATTN_WIDTH:ATTN_WIDTH + KV_WIDTH]).astype(BF16)
    v_ref[...] = proj[:, ATTN_WIDTH + KV_WIDTH:ATTN_WIDTH + 2 * KV_WIDTH].astype(BF16)
    c0 = ATTN_WIDTH + 2 * KV_WIDTH
    ga = proj[:, c0:c0 + CONV_CHANNELS]
    gb = proj[:, c0 + CONV_CHANNELS:c0 + 2 * CONV_CHANNELS]
    u_ref[...] = ga * jax.nn.sigmoid(gb)


def _inproj(x2, g, w_bf, cos_t, sin_t, seq):
    n = x2.shape[0]
    tm = TM_PROJ
    sb = seq // tm
    return pl.pallas_call(
        _inproj_kernel,
        grid=(n // tm,),
        in_specs=[
            pl.BlockSpec((tm, D_MODEL), lambda i: (i, 0)),
            pl.BlockSpec((1, D_MODEL), lambda i: (0, 0)),
            pl.BlockSpec((D_MODEL, IN_WIDTH), lambda i: (0, 0)),
            pl.BlockSpec((tm, LANES), lambda i: (i % sb, 0)),
            pl.BlockSpec((tm, LANES), lambda i: (i % sb, 0)),
        ],
        out_specs=[
            pl.BlockSpec((tm, ATTN_WIDTH), lambda i: (i, 0)),
            pl.BlockSpec((tm, KV_WIDTH), lambda i: (i, 0)),
            pl.BlockSpec((tm, KV_WIDTH), lambda i: (i, 0)),
            pl.BlockSpec((tm, CONV_CHANNELS), lambda i: (i, 0)),
        ],
        out_shape=[
            jax.ShapeDtypeStruct((n, ATTN_WIDTH), BF16),
            jax.ShapeDtypeStruct((n, KV_WIDTH), BF16),
            jax.ShapeDtypeStruct((n, KV_WIDTH), BF16),
            jax.ShapeDtypeStruct((n, CONV_CHANNELS), F32),
        ],
        compiler_params=_cparams(("arbitrary",)),
        name="inproj",
    )(x2, g, w_bf, cos_t, sin_t)


def _attn_kernel(sink_ref, q_ref, kp_ref, kc_ref, vp_ref, vc_ref, g_ref, o_ref):
    n = pl.program_id(1)
    q = q_ref[...]
    kcat = jnp.concatenate([kp_ref[...], kc_ref[...]], axis=0)
    vcat = jnp.concatenate([vp_ref[...], vc_ref[...]], axis=0)
    qi = lax.broadcasted_iota(jnp.int32, (WINDOW, 2 * WINDOW), 0)
    kj = lax.broadcasted_iota(jnp.int32, (WINDOW, 2 * WINDOW), 1)
    rel = qi + WINDOW - kj
    mask = (rel >= 0) & (rel < WINDOW) & ((kj >= WINDOW) | (n > 0))
    group = ATTN_HEADS // KV_HEADS
    outs = []
    for h in range(ATTN_HEADS):
        kv = h // group
        qh = q[:, h * HEAD_DIM:(h + 1) * HEAD_DIM]
        kg = kcat[:, kv * HEAD_DIM:(kv + 1) * HEAD_DIM]
        vg = vcat[:, kv * HEAD_DIM:(kv + 1) * HEAD_DIM]
        s = lax.dot_general(qh, kg, (((1,), (1,)), ((), ())), preferred_element_type=F32)
        s = jnp.where(mask, s, NEG)
        sink = sink_ref[h]
        m = jnp.maximum(jnp.max(s, axis=-1, keepdims=True), sink)
        p = jnp.exp(s - m)
        p = p / (jnp.sum(p, axis=-1, keepdims=True) + jnp.exp(sink - m))
        outs.append(jnp.dot(p.astype(BF16), vg, preferred_element_type=F32))
    attn = jnp.concatenate(outs, axis=1)
    o_ref[...] = _rms(attn, g_ref[...]).astype(BF16)


def _attention(sinks, q, k, v, g, batch, seq):
    n = q.shape[0]
    nb = seq // WINDOW
    cur = lambda b, i: (b * nb + i, 0)
    prev = lambda b, i: (b * nb + jnp.maximum(i - 1, 0), 0)
    return pl.pallas_call(
        _attn_kernel,
        grid=(batch, nb),
        in_specs=[
            pl.BlockSpec(memory_space=pltpu.SMEM),
            pl.BlockSpec((WINDOW, ATTN_WIDTH), cur),
            pl.BlockSpec((WINDOW, KV_WIDTH), prev),
            pl.BlockSpec((WINDOW, KV_WIDTH), cur),
            pl.BlockSpec((WINDOW, KV_WIDTH), prev),
            pl.BlockSpec((WINDOW, KV_WIDTH), cur),
            pl.BlockSpec((1, ATTN_WIDTH), lambda b, i: (0, 0)),
        ],
        out_specs=pl.BlockSpec((WINDOW, ATTN_WIDTH), cur),
        out_shape=jax.ShapeDtypeStruct((n, ATTN_WIDTH), BF16),
        compiler_params=_cparams(("arbitrary", "arbitrary")),
        name="swa",
    )(sinks, q, k, k, v, v, g)


def _conv_kernel(up_ref, uc_ref, w_ref, cb_ref, lg_ref, lb_ref, og_ref, o_ref, buf_ref):
    n = pl.program_id(1)
    ts = uc_ref.shape[0]
    tail = up_ref[ts - CONV_HALO:ts, :]
    buf_ref[0:CONV_HALO, :] = jnp.where(n > 0, tail, 0.0)
    buf_ref[CONV_HALO:CONV_HALO + ts, :] = uc_ref[...]
    off = CONV_HALO - (CONV_KERNEL - 1)
    for r0 in range(0, ts, CONV_SUB):
        acc = jnp.zeros((CONV_SUB, CONV_CHANNELS), F32)
        for k in range(CONV_KERNEL):
            acc = acc + w_ref[k:k + 1, :] * buf_ref[r0 + off + k:r0 + off + k + CONV_SUB, :]
        y = acc + cb_ref[...]
        mu = jnp.mean(y, axis=-1, keepdims=True)
        yc = y - mu
        z = yc * lax.rsqrt(jnp.mean(yc * yc, axis=-1, keepdims=True) + EPS) * lg_ref[...] + lb_ref[...]
        c = z * jax.nn.sigmoid(z)
        o_ref[r0:r0 + CONV_SUB, :] = _rms(c, og_ref[...]).astype(BF16)


def _conv(u, w, cb, lg, lb, og, batch, seq):
    n = u.shape[0]
    ts = TS_CONV
    nb = seq // ts
    cur = lambda b, i: (b * nb + i, 0)
    prev = lambda b, i: (b * nb + jnp.maximum(i - 1, 0), 0)
    vec = pl.BlockSpec((1, CONV_CHANNELS), lambda b, i: (0, 0))
    return pl.pallas_call(
        _conv_kernel,
        grid=(batch, nb),
        in_specs=[
            pl.BlockSpec((ts, CONV_CHANNELS), prev),
            pl.BlockSpec((ts, CONV_CHANNELS), cur),
            pl.BlockSpec((CONV_KERNEL, CONV_CHANNELS), lambda b, i: (0, 0)),
            vec, vec, vec, vec,
        ],
        out_specs=pl.BlockSpec((ts, CONV_CHANNELS), cur),
        out_shape=jax.ShapeDtypeStruct((n, CONV_CHANNELS), BF16),
        scratch_shapes=[pltpu.VMEM((CONV_HALO + ts, CONV_CHANNELS), F32)],
        compiler_params=_cparams(("arbitrary", "arbitrary")),
        name="conv",
    )(u, u, w, cb, lg, lb, og)


def _outproj_kernel(x_ref, ma_ref, mc_ref, wo_ref, g_ref, wq_ref, h_ref, hn_ref, pq_ref):
    h = (x_ref[...]
         + jnp.dot(ma_ref[...], wo_ref[0:ATTN_WIDTH, :], preferred_element_type=F32)
         + jnp.dot(mc_ref[...], wo_ref[ATTN_WIDTH:D_MODEL, :], preferred_element_type=F32))
    h_ref[...] = h
    hn = _rms(h, g_ref[...]).astype(BF16)
    hn_ref[...] = hn
    pq_ref[...] = jnp.dot(hn, wq_ref[...], preferred_element_type=F32).astype(BF16)


def _outproj(x2, ma, mc, wo_bf, g, wq_bf):
    n = x2.shape[0]
    tm = TM_PROJ
    qw = PEER_HEADS * PEER_QDIM
    row = lambda i: (i, 0)
    fixed = lambda i: (0, 0)
    return pl.pallas_call(
        _outproj_kernel,
        grid=(n // tm,),
        in_specs=[
            pl.BlockSpec((tm, D_MODEL), row),
            pl.BlockSpec((tm, ATTN_WIDTH), row),
            pl.BlockSpec((tm, CONV_CHANNELS), row),
            pl.BlockSpec((D_MODEL, D_MODEL), fixed),
            pl.BlockSpec((1, D_MODEL), fixed),
            pl.BlockSpec((D_MODEL, qw), fixed),
        ],
        out_specs=[
            pl.BlockSpec((tm, D_MODEL), row),
            pl.BlockSpec((tm, D_MODEL), row),
            pl.BlockSpec((tm, qw), row),
        ],
        out_shape=[
            jax.ShapeDtypeStruct((n, D_MODEL), F32),
            jax.ShapeDtypeStruct((n, D_MODEL), BF16),
            jax.ShapeDtypeStruct((n, qw), BF16),
        ],
        compiler_params=_cparams(("arbitrary",)),
        name="outproj",
    )(x2, ma, mc, wo_bf, g, wq_bf)


def _topk_rows(s, row_id, count, payload=None):
    big = jnp.int32(2 ** 30)
    vals, ids, pays = [], [], []
    for _ in range(count):
        m = jnp.max(s, axis=0, keepdims=True)
        win = jnp.min(jnp.where(s == m, row_id, big), axis=0, keepdims=True)
        hit = row_id == win
        if payload is not None:
            pays.append(jnp.max(jnp.where(hit, payload, -1), axis=0, keepdims=True))
        s = jnp.where(hit, -jnp.inf, s)
        vals.append(m)
        ids.append(win)
    return vals, ids, pays


def _route_kernel(pq_ref, keys_ref, ids_ref, gate_ref, ids_scr, gate_scr):
    tb = pq_ref.shape[0]
    key_row = lax.broadcasted_iota(jnp.int32, (N_KEYS, tb), 0)
    r8 = lax.broadcasted_iota(jnp.int32, (SUBLANES, tb), 0)
    half = PEER_QDIM // 2

    def head(h, carry):
        tops = []
        for p in range(2):
            hp = h * 2 + p
            qhp = pq_ref[:, pl.ds(pl.multiple_of(hp * half, half), half)]
            s = lax.dot_general(keys_ref[hp], qhp, (((1,), (1,)), ((), ())),
                                preferred_element_type=F32)
            vals, idx, _ = _topk_rows(s, key_row, PEER_TOPK)
            tops.append((vals, idx))
        (va, ia), (vb, ib) = tops
        lo = SUBLANES
        vb_lo = jnp.concatenate(vb[:lo], axis=0)
        ib_lo = jnp.concatenate(ib[:lo], axis=0)
        cand, cid, eid = [], [], []
        for k1 in range(lo):
            ok = (k1 + 1) * (r8 + 1) <= PEER_TOPK
            cand.append(jnp.where(ok, va[k1] + vb_lo, -jnp.inf))
            cid.append(k1 * PEER_TOPK + r8)
            eid.append(ia[k1] * N_KEYS + ib_lo)
        cand.append(va[0] + jnp.concatenate(vb[lo:], axis=0))
        cid.append(lo + r8)
        eid.append(ia[0] * N_KEYS + jnp.concatenate(ib[lo:], axis=0))
        cand.append(jnp.concatenate(va[lo:], axis=0) + vb[0])
        cid.append((lo + r8) * PEER_TOPK)
        eid.append(jnp.concatenate(ia[lo:], axis=0) * N_KEYS + ib[0])
        cand = jnp.concatenate(cand, axis=0)
        cid = jnp.concatenate(cid, axis=0)
        eid = jnp.concatenate(eid, axis=0)
        best, _, experts = _topk_rows(cand, cid, PEER_TOPK, payload=eid)
        e = [jnp.exp(b - best[0]) for b in best]
        tot = e[0]
        for x in e[1:]:
            tot = tot + x
        row0 = pl.multiple_of(h * PEER_TOPK, PEER_TOPK)
        gate_scr[pl.ds(row0, PEER_TOPK), :] = jnp.concatenate([x / tot for x in e], axis=0)
        ids_scr[pl.ds(row0, PEER_TOPK), :] = jnp.concatenate(experts, axis=0)
        return carry

    lax.fori_loop(0, PEER_HEADS, head, 0)
    ids_ref[...] = ids_scr[...].T
    gate_ref[...] = gate_scr[...].T


def _route(pq, keys_bf):
    n = pq.shape[0]
    tb = TB_ROUTE
    nsel = PEER_HEADS * PEER_TOPK
    return pl.pallas_call(
        _route_kernel,
        grid=(n // tb,),
        in_specs=[
            pl.BlockSpec((tb, PEER_HEADS * PEER_QDIM), lambda i: (i, 0)),
            pl.BlockSpec((2 * PEER_HEADS, N_KEYS, PEER_QDIM // 2), lambda i: (0, 0, 0)),
        ],
        out_specs=[
            pl.BlockSpec((tb, nsel), lambda i: (i, 0)),
            pl.BlockSpec((tb, nsel), lambda i: (i, 0)),
        ],
        out_shape=[
            jax.ShapeDtypeStruct((n, nsel), jnp.int32),
            jax.ShapeDtypeStruct((n, nsel), F32),
        ],
        scratch_shapes=[pltpu.VMEM((nsel, tb), jnp.int32), pltpu.VMEM((nsel, tb), F32)],
        compiler_params=_cparams(("arbitrary",)),
        name="peer_route",
    )(pq, keys_bf)


def _peer_kernel(ids_ref, gate_ref, hn_ref, h_ref, ut_ref, v_ref, fg_ref, o_ref, gs_ref, w_ref, acc_ref, *,
                 final_norm):
    c = pl.program_id(1)
    tb = hn_ref.shape[0]

    @pl.when(c == 0)
    def _scatter():
        acc_ref[...] = jnp.zeros_like(acc_ref)
        sub = lax.broadcasted_iota(jnp.int32, (N_KEYS, ids_ref.shape[1]), 0)

        def tokens(i, carry):
            for j in range(SCATTER_UNROLL):
                t = i * SCATTER_UNROLL + j
                ids = ids_ref[pl.ds(t, 1), :]
                gate = gate_ref[pl.ds(t, 1), :]
                k1 = lax.shift_right_logical(ids, 7)
                k2 = ids & (N_KEYS - 1)
                m1 = jnp.where(sub == k1, gate, 0.0).astype(BF16)
                m2 = jnp.where(sub == k2, 1.0, 0.0).astype(BF16)
                tile = lax.dot_general(m1, m2, (((1,), (1,)), ((), ())), preferred_element_type=F32)
                gs_ref[pl.ds(t, N_KEYS, stride=GATE_PITCH), :] = tile
            return carry

        lax.fori_loop(0, tb // SCATTER_UNROLL, tokens, 0)

    hn = hn_ref[...]
    keys_per_sub = ESUB_PEER // N_KEYS
    for sc in range(EC_PEER // ESUB_PEER):
        sl = slice(sc * ESUB_PEER, (sc + 1) * ESUB_PEER)
        act = jnp.dot(hn, ut_ref[:, sl], preferred_element_type=F32)
        k1_0 = c * (EC_PEER // N_KEYS) + sc * keys_per_sub
        gates = jnp.concatenate(
            [gs_ref[pl.ds(pl.multiple_of((k1_0 + i) * GATE_PITCH, SUBLANES), tb), :] for i in range(keys_per_sub)],
            axis=1)
        gelu = 0.5 * act * (1.0 + lax.erf(act * (2.0 ** -0.5)))
        w_ref[:, sl] = (gelu * gates).astype(BF16)
    acc_ref[...] += jnp.dot(w_ref[...], v_ref[...], preferred_element_type=F32)

    @pl.when(c == pl.num_programs(1) - 1)
    def _finish():
        out = h_ref[...] + acc_ref[...]
        o_ref[...] = _rms(out, fg_ref[...]) if final_norm else out


def _peer(ids, gate, hn, h, ut_bf, v_bf, fg, final_norm):
    n = hn.shape[0]
    tb = TB_PEER
    nsel = PEER_HEADS * PEER_TOPK
    tok = lambda i, c: (i, 0)
    return pl.pallas_call(
        functools.partial(_peer_kernel, final_norm=final_norm),
        grid=(n // tb, N_EXPERTS // EC_PEER),
        in_specs=[
            pl.BlockSpec((tb, nsel), tok),
            pl.BlockSpec((tb, nsel), tok),
            pl.BlockSpec((tb, D_MODEL), tok),
            pl.BlockSpec((tb, D_MODEL), tok),
            pl.BlockSpec((D_MODEL, EC_PEER), lambda i, c: (0, c)),
            pl.BlockSpec((EC_PEER, D_MODEL), lambda i, c: (c, 0)),
            pl.BlockSpec((1, D_MODEL), lambda i, c: (0, 0)),
        ],
        out_specs=pl.BlockSpec((tb, D_MODEL), tok),
        out_shape=jax.ShapeDtypeStruct((n, D_MODEL), F32),
        scratch_shapes=[
            pltpu.VMEM((N_KEYS * GATE_PITCH, N_KEYS), F32),
            pltpu.VMEM((tb, EC_PEER), BF16),
            pltpu.VMEM((tb, D_MODEL), F32),
        ],
        compiler_params=_cparams(("arbitrary", "arbitrary")),
        name="peer_experts",
    )(ids, gate, hn, h, ut_bf, v_bf, fg)


def _rope_tables(seq):
    half = HEAD_DIM // 2
    inv = ROPE_THETA ** (-jnp.arange(half, dtype=F32) / half)
    ang = jnp.arange(seq, dtype=jnp.int32).astype(F32)[:, None] * inv[None, :]
    cos = jnp.cos(ang)
    sin = jnp.sin(ang)
    reps = LANES // HEAD_DIM
    cos_t = jnp.tile(jnp.concatenate([cos, cos], axis=1), (1, reps))
    sin_t = jnp.tile(jnp.concatenate([-sin, sin], axis=1), (1, reps))
    return cos_t, sin_t


def kernel(x, norm_mix_g, w_in, conv_w, conv_b, conv_ln_g, conv_ln_b, attn_sinks, attn_out_g, conv_out_g, w_out,
           norm_ffn_g, peer_w_q, peer_sub_keys, peer_u, peer_v, final_norm_g):
    batch, seq, _ = x.shape
    depth = w_in.shape[0]
    cos_t, sin_t = _rope_tables(seq)
    h = x.reshape(batch * seq, D_MODEL)
    row = lambda a: a.reshape(1, -1)
    for l in range(depth):
        q, k, v, u = _inproj(h, row(norm_mix_g[l]), w_in[l].astype(BF16), cos_t, sin_t, seq)
        ma = _attention(attn_sinks[l], q, k, v, row(attn_out_g[l]), batch, seq)
        mc = _conv(u, conv_w[l], row(conv_b[l]), row(conv_ln_g[l]), row(conv_ln_b[l]), row(conv_out_g[l]),
                   batch, seq)
        h, hn, pq = _outproj(h, ma, mc, w_out[l].astype(BF16), row(norm_ffn_g[l]), peer_w_q[l].astype(BF16))
        keys = peer_sub_keys[l].reshape(2 * PEER_HEADS, N_KEYS, PEER_QDIM // 2).astype(BF16)
        ids, gate = _route(pq, keys)
        h = _peer(ids, gate, hn, h, peer_u[l].T.astype(BF16), peer_v[l].astype(BF16), row(final_norm_g),
                  final_norm=(l == depth - 1))
    return h.reshape(batch, seq, D_MODEL)
```

```python
import functools

import jax
import jax.numpy as jnp
from jax import lax
from jax.experimental import pallas as pl
from jax.experimental.pallas import tpu as pltpu

F32 = jnp.float32
BF16 = jnp.bfloat16

D_MODEL = 1024
ATTN_HEADS = 8
KV_HEADS = 2
HEAD_DIM = 64
ATTN_WIDTH = ATTN_HEADS * HEAD_DIM
KV_WIDTH = KV_HEADS * HEAD_DIM
CONV_CHANNELS = D_MODEL - ATTN_WIDTH
IN_WIDTH = ATTN_WIDTH + 2 * KV_WIDTH + 2 * CONV_CHANNELS
CONV_KERNEL = 31
WINDOW = 128
ROPE_THETA = 10000.0
PEER_HEADS = 8
N_KEYS = 128
N_EXPERTS = N_KEYS * N_KEYS
PEER_QDIM = 256
PEER_TOPK = 16
EPS = 1e-6
NEG = -1e30

LANES = 128
SUBLANES = 8
VMEM_LIMIT = 56 * 1024 * 1024

TM_PROJ = 512
TS_CONV = 256
CONV_HALO = 32
CONV_SUB = 64
TB_PEER = 256
EC_PEER = 2048
ESUB_PEER = 512
SCATTER_UNROLL = 32
ROUTE_PAUSE = 4
ROUTE_SEGMENTS = 3 * PEER_TOPK // ROUTE_PAUSE
ROWS_PEER = 256
GATE_PITCH = TB_PEER + SUBLANES


def _rms(x, g):
    return x * lax.rsqrt(jnp.mean(x * x, axis=-1, keepdims=True) + EPS) * g


def _cparams(sem):
    return pltpu.CompilerParams(dimension_semantics=sem, vmem_limit_bytes=VMEM_LIMIT)


def _inproj_kernel(x_ref, g_ref, w_ref, cos_ref, sin_ref, q_ref, k_ref, v_ref, u_ref):
    hn = _rms(x_ref[...], g_ref[...]).astype(BF16)
    proj = jnp.dot(hn, w_ref[...], preferred_element_type=F32)
    cos = cos_ref[...]
    sin = sin_ref[...]
    lane = lax.broadcasted_iota(jnp.int32, cos.shape, 1)
    first_half = (lane % HEAD_DIM) < (HEAD_DIM // 2)

    def rope(z):
        partner = jnp.where(first_half,
                            pltpu.roll(z, LANES - HEAD_DIM // 2, 1),
                            pltpu.roll(z, HEAD_DIM // 2, 1))
        return z * cos + partner * sin

    scale = HEAD_DIM ** -0.5
    for c in range(ATTN_WIDTH // LANES):
        sl = slice(c * LANES, (c + 1) * LANES)
        q_ref[:, sl] = (rope(proj[:, sl]) * scale).astype(BF16)
    k_ref[...] = rope(proj[:, ATTN_WIDTH:ATTN_WIDTH + KV_WIDTH]).astype(BF16)
    v_ref[...] = proj[:, ATTN_WIDTH + KV_WIDTH:ATTN_WIDTH + 2 * KV_WIDTH].astype(BF16)
    c0 = ATTN_WIDTH + 2 * KV_WIDTH
    ga = proj[:, c0:c0 + CONV_CHANNELS]
    gb = proj[:, c0 + CONV_CHANNELS:c0 + 2 * CONV_CHANNELS]
    u_ref[...] = ga * jax.nn.sigmoid(gb)


def _inproj(x2, g, w_bf, cos_t, sin_t, seq):
    n = x2.shape[0]
    tm = TM_PROJ
    sb = seq // tm
    return pl.pallas_call(
        _inproj_kernel,
        grid=(n // tm,),
        in_specs=[
            pl.BlockSpec((tm, D_MODEL), lambda i: (i, 0)),
            pl.BlockSpec((1, D_MODEL), lambda i: (0, 0)),
            pl.BlockSpec((D_MODEL, IN_WIDTH), lambda i: (0, 0)),
            pl.BlockSpec((tm, LANES), lambda i: (i % sb, 0)),
            pl.BlockSpec((tm, LANES), lambda i: (i % sb, 0)),
        ],
        out_specs=[
            pl.BlockSpec((tm, ATTN_WIDTH), lambda i: (i, 0)),
            pl.BlockSpec((tm, KV_WIDTH), lambda i: (i, 0)),
            pl.BlockSpec((tm, KV_WIDTH), lambda i: (i, 0)),
            pl.BlockSpec((tm, CONV_CHANNELS), lambda i: (i, 0)),
        ],
        out_shape=[
            jax.ShapeDtypeStruct((n, ATTN_WIDTH), BF16),
            jax.ShapeDtypeStruct((n, KV_WIDTH), BF16),
            jax.ShapeDtypeStruct((n, KV_WIDTH), BF16),
            jax.ShapeDtypeStruct((n, CONV_CHANNELS), F32),
        ],
        compiler_params=_cparams(("arbitrary",)),
        name="inproj",
    )(x2, g, w_bf, cos_t, sin_t)


def _attn_kernel(sink_ref, q_ref, kp_ref, kc_ref, vp_ref, vc_ref, g_ref, o_ref):
    n = pl.program_id(1)
    q = q_ref[...]
    kcat = jnp.concatenate([kp_ref[...], kc_ref[...]], axis=0)
    vcat = jnp.concatenate([vp_ref[...], vc_ref[...]], axis=0)
    qi = lax.broadcasted_iota(jnp.int32, (WINDOW, 2 * WINDOW), 0)
    kj = lax.broadcasted_iota(jnp.int32, (WINDOW, 2 * WINDOW), 1)
    rel = qi + WINDOW - kj
    mask = (rel >= 0) & (rel < WINDOW) & ((kj >= WINDOW) | (n > 0))
    group = ATTN_HEADS // KV_HEADS
    outs = []
    for h in range(ATTN_HEADS):
        kv = h // group
        qh = q[:, h * HEAD_DIM:(h + 1) * HEAD_DIM]
        kg = kcat[:, kv * HEAD_DIM:(kv + 1) * HEAD_DIM]
        vg = vcat[:, kv * HEAD_DIM:(kv + 1) * HEAD_DIM]
        s = lax.dot_general(qh, kg, (((1,), (1,)), ((), ())), preferred_element_type=F32)
        s = jnp.where(mask, s, NEG)
        sink = sink_ref[h]
        m = jnp.maximum(jnp.max(s, axis=-1, keepdims=True), sink)
        p = jnp.exp(s - m)
        p = p / (jnp.sum(p, axis=-1, keepdims=True) + jnp.exp(sink - m))
        outs.append(jnp.dot(p.astype(BF16), vg, preferred_element_type=F32))
    attn = jnp.concatenate(outs, axis=1)
    o_ref[...] = _rms(attn, g_ref[...]).astype(BF16)


def _attention(sinks, q, k, v, g, batch, seq):
    n = q.shape[0]
    nb = seq // WINDOW
    cur = lambda b, i: (b * nb + i, 0)
    prev = lambda b, i: (b * nb + jnp.maximum(i - 1, 0), 0)
    return pl.pallas_call(
        _attn_kernel,
        grid=(batch, nb),
        in_specs=[
            pl.BlockSpec(memory_space=pltpu.SMEM),
            pl.BlockSpec((WINDOW, ATTN_WIDTH), cur),
            pl.BlockSpec((WINDOW, KV_WIDTH), prev),
            pl.BlockSpec((WINDOW, KV_WIDTH), cur),
            pl.BlockSpec((WINDOW, KV_WIDTH), prev),
            pl.BlockSpec((WINDOW, KV_WIDTH), cur),
            pl.BlockSpec((1, ATTN_WIDTH), lambda b, i: (0, 0)),
        ],
        out_specs=pl.BlockSpec((WINDOW, ATTN_WIDTH), cur),
        out_shape=jax.ShapeDtypeStruct((n, ATTN_WIDTH), BF16),
        compiler_params=_cparams(("arbitrary", "arbitrary")),
        name="swa",
    )(sinks, q, k, k, v, v, g)


def _conv_kernel(up_ref, uc_ref, w_ref, cb_ref, lg_ref, lb_ref, og_ref, o_ref, buf_ref):
    n = pl.program_id(1)
    ts = uc_ref.shape[0]
    tail = up_ref[ts - CONV_HALO:ts, :]
    buf_ref[0:CONV_HALO, :] = jnp.where(n > 0, tail, 0.0)
    buf_ref[CONV_HALO:CONV_HALO + ts, :] = uc_ref[...]
    off = CONV_HALO - (CONV_KERNEL - 1)
    for r0 in range(0, ts, CONV_SUB):
        acc = jnp.zeros((CONV_SUB, CONV_CHANNELS), F32)
        for k in range(CONV_KERNEL):
            acc = acc + w_ref[k:k + 1, :] * buf_ref[r0 + off + k:r0 + off + k + CONV_SUB, :]
        y = acc + cb_ref[...]
        mu = jnp.mean(y, axis=-1, keepdims=True)
        yc = y - mu
        z = yc * lax.rsqrt(jnp.mean(yc * yc, axis=-1, keepdims=True) + EPS) * lg_ref[...] + lb_ref[...]
        c = z * jax.nn.sigmoid(z)
        o_ref[r0:r0 + CONV_SUB, :] = _rms(c, og_ref[...]).astype(BF16)


def _conv(u, w, cb, lg, lb, og, batch, seq):
    n = u.shape[0]
    ts = TS_CONV
    nb = seq // ts
    cur = lambda b, i: (b * nb + i, 0)
    prev = lambda b, i: (b * nb + jnp.maximum(i - 1, 0), 0)
    vec = pl.BlockSpec((1, CONV_CHANNELS), lambda b, i: (0, 0))
    return pl.pallas_call(
        _conv_kernel,
        grid=(batch, nb),
        in_specs=[
            pl.BlockSpec((ts, CONV_CHANNELS), prev),
            pl.BlockSpec((ts, CONV_CHANNELS), cur),
            pl.BlockSpec((CONV_KERNEL, CONV_CHANNELS), lambda b, i: (0, 0)),
            vec, vec, vec, vec,
        ],
        out_specs=pl.BlockSpec((ts, CONV_CHANNELS), cur),
        out_shape=jax.ShapeDtypeStruct((n, CONV_CHANNELS), BF16),
        scratch_shapes=[pltpu.VMEM((CONV_HALO + ts, CONV_CHANNELS), F32)],
        compiler_params=_cparams(("arbitrary", "arbitrary")),
        name="conv",
    )(u, u, w, cb, lg, lb, og)


def _outproj_kernel(x_ref, ma_ref, mc_ref, wo_ref, g_ref, wq_ref, h_ref, hn_ref, pq_ref):
    h = (x_ref[...]
         + jnp.dot(ma_ref[...], wo_ref[0:ATTN_WIDTH, :], preferred_element_type=F32)
         + jnp.dot(mc_ref[...], wo_ref[ATTN_WIDTH:D_MODEL, :], preferred_element_type=F32))
    h_ref[...] = h
    hn = _rms(h, g_ref[...]).astype(BF16)
    hn_ref[...] = hn
    pq_ref[...] = jnp.dot(hn, wq_ref[...], preferred_element_type=F32).astype(BF16)


def _outproj(x2, ma, mc, wo_bf, g, wq_bf):
    n = x2.shape[0]
    tm = TM_PROJ
    qw = PEER_HEADS * PEER_QDIM
    row = lambda i: (i, 0)
    fixed = lambda i: (0, 0)
    return pl.pallas_call(
        _outproj_kernel,
        grid=(n // tm,),
        in_specs=[
            pl.BlockSpec((tm, D_MODEL), row),
            pl.BlockSpec((tm, ATTN_WIDTH), row),
            pl.BlockSpec((tm, CONV_CHANNELS), row),
            pl.BlockSpec((D_MODEL, D_MODEL), fixed),
            pl.BlockSpec((1, D_MODEL), fixed),
            pl.BlockSpec((D_MODEL, qw), fixed),
        ],
        out_specs=[
            pl.BlockSpec((tm, D_MODEL), row),
            pl.BlockSpec((tm, D_MODEL), row),
            pl.BlockSpec((tm, qw), row),
        ],
        out_shape=[
            jax.ShapeDtypeStruct((n, D_MODEL), F32),
            jax.ShapeDtypeStruct((n, D_MODEL), BF16),
            jax.ShapeDtypeStruct((n, qw), BF16),
        ],
        compiler_params=_cparams(("arbitrary",)),
        name="outproj",
    )(x2, ma, mc, wo_bf, g, wq_bf)


def _topk_rows(s, row_id, count, payload=None):
    big = jnp.int32(2 ** 30)
    vals, ids, pays = [], [], []
    for k in range(count):
        m = jnp.max(s, axis=0, keepdims=True)
        win = jnp.min(jnp.where(s == m, row_id, big), axis=0, keepdims=True)
        hit = row_id == win
        if payload is not None:
            pays.append(jnp.max(jnp.where(hit, payload, -1), axis=0, keepdims=True))
        s = jnp.where(hit, -jnp.inf, s)
        vals.append(m)
        ids.append(win)
        if (k + 1) % ROUTE_PAUSE == 0:
            yield
    return vals, ids, pays


def _subkey_scores(pq_ref, keys_ref, h, p):
    half = PEER_QDIM // 2
    hp = h * 2 + p
    qhp = pq_ref[:, pl.ds(pl.multiple_of(hp * half, half), half)]
    return lax.dot_general(keys_ref[hp], qhp, (((1,), (1,)), ((), ())), preferred_element_type=F32)


def _route_subkeys(pq_ref, keys_ref, h, p):
    s = _subkey_scores(pq_ref, keys_ref, h, p)
    key_row = lax.broadcasted_iota(jnp.int32, s.shape, 0)
    vals, idx, _ = yield from _topk_rows(s, key_row, PEER_TOPK)
    return vals, idx


def _route_pairs(top_a, top_b, h, ids_scr, gate_scr):
    (va, ia), (vb, ib) = top_a, top_b
    r8 = lax.broadcasted_iota(jnp.int32, (SUBLANES, va[0].shape[1]), 0)
    lo = SUBLANES
    vb_lo = jnp.concatenate(vb[:lo], axis=0)
    ib_lo = jnp.concatenate(ib[:lo], axis=0)
    cand, cid, eid = [], [], []
    for k1 in range(lo):
        ok = (k1 + 1) * (r8 + 1) <= PEER_TOPK
        cand.append(jnp.where(ok, va[k1] + vb_lo, -jnp.inf))
        cid.append(k1 * PEER_TOPK + r8)
        eid.append(ia[k1] * N_KEYS + ib_lo)
    cand.append(va[0] + jnp.concatenate(vb[lo:], axis=0))
    cid.append(lo + r8)
    eid.append(ia[0] * N_KEYS + jnp.concatenate(ib[lo:], axis=0))
    cand.append(jnp.concatenate(va[lo:], axis=0) + vb[0])
    cid.append((lo + r8) * PEER_TOPK)
    eid.append(jnp.concatenate(ia[lo:], axis=0) * N_KEYS + ib[0])
    cand = jnp.concatenate(cand, axis=0)
    cid = jnp.concatenate(cid, axis=0)
    eid = jnp.concatenate(eid, axis=0)
    best, _, experts = yield from _topk_rows(cand, cid, PEER_TOPK, payload=eid)
    e = [jnp.exp(b - best[0]) for b in best]
    tot = e[0]
    for x in e[1:]:
        tot = tot + x
    row0 = pl.multiple_of(h * PEER_TOPK, PEER_TOPK)
    gate_scr[pl.ds(row0, PEER_TOPK), :] = jnp.concatenate([x / tot * 0.5 for x in e], axis=0)
    ids_scr[pl.ds(row0, PEER_TOPK), :] = jnp.concatenate(experts, axis=0)


def _route_head_steps(pq_ref, keys_ref, h, ids_scr, gate_scr):
    top_a = yield from _route_subkeys(pq_ref, keys_ref, h, 0)
    top_b = yield from _route_subkeys(pq_ref, keys_ref, h, 1)
    yield from _route_pairs(top_a, top_b, h, ids_scr, gate_scr)


def _route_head(pq_ref, keys_ref, h, ids_scr, gate_scr):
    for _ in _route_head_steps(pq_ref, keys_ref, h, ids_scr, gate_scr):
        pass


def _peer_kernel(pq0_ref, pqn_ref, keys_ref, hn_ref, h_ref, ut_ref, v_ref, fg_ref, o_ref,
                 gs_ref, w_ref, acc_ref, ids_scr, gate_scr, ids_rows, gate_rows, *, final_norm):
    i = pl.program_id(0)
    c = pl.program_id(1)
    tb = hn_ref.shape[0]

    @pl.when((i == 0) & (c == 0))
    def _route_first_block():
        def head(h, carry):
            _route_head(pq0_ref, keys_ref, h, ids_scr, gate_scr)
            return carry
        lax.fori_loop(0, PEER_HEADS, head, 0)

    @pl.when(c == 0)
    def _scatter():
        acc_ref[...] = jnp.zeros_like(acc_ref)
        ids_rows[...] = ids_scr[...].T
        gate_rows[...] = gate_scr[...].T
        sub = lax.broadcasted_iota(jnp.int32, (N_KEYS, ids_rows.shape[1]), 0)

        def tokens(it, carry):
            for j in range(SCATTER_UNROLL):
                t = it * SCATTER_UNROLL + j
                ids = ids_rows[pl.ds(t, 1), :]
                gate = gate_rows[pl.ds(t, 1), :]
                k1 = lax.shift_right_logical(ids, 7)
                k2 = ids & (N_KEYS - 1)
                m1 = jnp.where(sub == k1, gate, 0.0).astype(BF16)
                m2 = jnp.where(sub == k2, 1.0, 0.0).astype(BF16)
                tile = lax.dot_general(m1, m2, (((1,), (1,)), ((), ())), preferred_element_type=F32)
                gs_ref[pl.ds(t, N_KEYS, stride=GATE_PITCH), :] = tile
            return carry

        lax.fori_loop(0, tb // SCATTER_UNROLL, tokens, 0)

    keys_per_sub = ESUB_PEER // N_KEYS

    def expert_weights(sc, r0):
        rows = slice(r0, r0 + ROWS_PEER)
        sl = slice(sc * ESUB_PEER, (sc + 1) * ESUB_PEER)
        ut = pltpu.bitcast(ut_ref[:, sl], BF16)
        act = jnp.dot(hn_ref[rows, :], ut, preferred_element_type=F32)
        k1_0 = c * (EC_PEER // N_KEYS) + sc * keys_per_sub
        gates = jnp.concatenate(
            [gs_ref[pl.ds(pl.multiple_of((k1_0 + a) * GATE_PITCH + r0, SUBLANES), ROWS_PEER), :]
             for a in range(keys_per_sub)], axis=1)
        w_ref[rows, sl] = (act * (1.0 + lax.erf(act * (2.0 ** -0.5))) * gates).astype(BF16)

    def expert_outputs(sc, r0):
        rows = slice(r0, r0 + ROWS_PEER)
        v = pltpu.bitcast(v_ref[sc * ESUB_PEER // 2:(sc + 1) * ESUB_PEER // 2, :], BF16)
        acc_ref[rows, :] += jnp.dot(w_ref[rows, sc * ESUB_PEER:(sc + 1) * ESUB_PEER], v,
                                    preferred_element_type=F32)

    mxu_work = []
    for sc in range(EC_PEER // ESUB_PEER):
        for r0 in range(0, tb, ROWS_PEER):
            mxu_work.append(functools.partial(expert_weights, sc, r0))
        if sc > 0:
            for r0 in range(0, tb, ROWS_PEER):
                mxu_work.append(functools.partial(expert_outputs, sc - 1, r0))
    for r0 in range(0, tb, ROWS_PEER):
        mxu_work.append(functools.partial(expert_outputs, EC_PEER // ESUB_PEER - 1, r0))
    routing = _route_head_steps(pqn_ref, keys_ref, c, ids_scr, gate_scr)
    issued = 0
    for seg in range(ROUTE_SEGMENTS):
        while issued * ROUTE_SEGMENTS < (seg + 1) * len(mxu_work):
            mxu_work[issued]()
            issued += 1
        next(routing)
    for _ in routing:
        pass

    @pl.when(c == pl.num_programs(1) - 1)
    def _finish():
        out = h_ref[...] + acc_ref[...]
        o_ref[...] = _rms(out, fg_ref[...]) if final_norm else out


def _peer(pq, keys_bf, hn, h, ut_bf, v_bf, fg, final_norm):
    n = hn.shape[0]
    tb = TB_PEER
    nblk = n // tb
    nsel = PEER_HEADS * PEER_TOPK
    qw = PEER_HEADS * PEER_QDIM
    assert N_EXPERTS // EC_PEER == PEER_HEADS
    tok = lambda i, c: (i, 0)
    return pl.pallas_call(
        functools.partial(_peer_kernel, final_norm=final_norm),
        grid=(nblk, N_EXPERTS // EC_PEER),
        in_specs=[
            pl.BlockSpec((tb, qw), lambda i, c: (0, 0)),
            pl.BlockSpec((tb, qw), lambda i, c: (jnp.minimum(i + 1, nblk - 1), 0)),
            pl.BlockSpec((2 * PEER_HEADS, N_KEYS, PEER_QDIM // 2), lambda i, c: (0, 0, 0)),
            pl.BlockSpec((tb, D_MODEL), tok),
            pl.BlockSpec((tb, D_MODEL), tok),
            pl.BlockSpec((D_MODEL // 2, EC_PEER), lambda i, c: (0, c)),
            pl.BlockSpec((EC_PEER // 2, D_MODEL), lambda i, c: (c, 0)),
            pl.BlockSpec((1, D_MODEL), lambda i, c: (0, 0)),
        ],
        out_specs=pl.BlockSpec((tb, D_MODEL), tok),
        out_shape=jax.ShapeDtypeStruct((n, D_MODEL), F32),
        scratch_shapes=[
            pltpu.VMEM((N_KEYS * GATE_PITCH, N_KEYS), F32),
            pltpu.VMEM((tb, EC_PEER), BF16),
            pltpu.VMEM((tb, D_MODEL), F32),
            pltpu.VMEM((nsel, tb), jnp.int32),
            pltpu.VMEM((nsel, tb), F32),
            pltpu.VMEM((tb, nsel), jnp.int32),
            pltpu.VMEM((tb, nsel), F32),
        ],
        compiler_params=_cparams(("arbitrary", "arbitrary")),
        name="peer_experts",
    )(pq, pq, keys_bf, hn, h, ut_bf, v_bf, fg)


def _rope_tables(seq):
    half = HEAD_DIM // 2
    inv = ROPE_THETA ** (-jnp.arange(half, dtype=F32) / half)
    ang = jnp.arange(seq, dtype=jnp.int32).astype(F32)[:, None] * inv[None, :]
    cos = jnp.cos(ang)
    sin = jnp.sin(ang)
    reps = LANES // HEAD_DIM
    cos_t = jnp.tile(jnp.concatenate([cos, cos], axis=1), (1, reps))
    sin_t = jnp.tile(jnp.concatenate([-sin, sin], axis=1), (1, reps))
    return cos_t, sin_t


def _pack_row_pairs(w_bf):
    r, c = w_bf.shape
    return lax.bitcast_convert_type(w_bf.reshape(r // 2, 2, c).transpose(0, 2, 1), jnp.uint32)


def kernel(x, norm_mix_g, w_in, conv_w, conv_b, conv_ln_g, conv_ln_b, attn_sinks, attn_out_g, conv_out_g, w_out,
           norm_ffn_g, peer_w_q, peer_sub_keys, peer_u, peer_v, final_norm_g):
    batch, seq, _ = x.shape
    depth = w_in.shape[0]
    cos_t, sin_t = _rope_tables(seq)
    h = x.reshape(batch * seq, D_MODEL)
    row = lambda a: a.reshape(1, -1)
    for l in range(depth):
        q, k, v, u = _inproj(h, row(norm_mix_g[l]), w_in[l].astype(BF16), cos_t, sin_t, seq)
        ma = _attention(attn_sinks[l], q, k, v, row(attn_out_g[l]), batch, seq)
        mc = _conv(u, conv_w[l], row(conv_b[l]), row(conv_ln_g[l]), row(conv_ln_b[l]), row(conv_out_g[l]),
                   batch, seq)
        h, hn, pq = _outproj(h, ma, mc, w_out[l].astype(BF16), row(norm_ffn_g[l]), peer_w_q[l].astype(BF16))
        keys = peer_sub_keys[l].reshape(2 * PEER_HEADS, N_KEYS, PEER_QDIM // 2).astype(BF16)
        h = _peer(pq, keys, hn, h, _pack_row_pairs(peer_u[l].T.astype(BF16)), _pack_row_pairs(peer_v[l].astype(BF16)),
                  row(final_norm_g), final_norm=(l == depth - 1))
    return h.reshape(batch, seq, D_MODEL)
```

```python
import functools

import jax
import jax.numpy as jnp
from jax import lax
from jax.experimental import pallas as pl
from jax.experimental.pallas import tpu as pltpu

F32 = jnp.float32
BF16 = jnp.bfloat16

D_MODEL = 1024
ATTN_HEADS = 8
KV_HEADS = 2
HEAD_DIM = 64
ATTN_WIDTH = ATTN_HEADS * HEAD_DIM
KV_WIDTH = KV_HEADS * HEAD_DIM
CONV_CHANNELS = D_MODEL - ATTN_WIDTH
IN_WIDTH = ATTN_WIDTH + 2 * KV_WIDTH + 2 * CONV_CHANNELS
CONV_KERNEL = 31
WINDOW = 128
ROPE_THETA = 10000.0
PEER_HEADS = 8
N_KEYS = 128
KEY_BITS = N_KEYS.bit_length() - 1
N_EXPERTS = N_KEYS * N_KEYS
PEER_QDIM = 256
PEER_TOPK = 16
EPS = 1e-6
NEG = -1e30

LANES = 128
SUBLANES = 8
VMEM_LIMIT = 60 * 1024 * 1024

TM_PROJ = 1024
SWA_BLOCK = 256
TS_CONV = 512
CONV_HALO = 32
CONV_SUB = 64
TB_PEER = 512
EC_PEER = 2048
ESUB_PEER = 512
SCATTER_UNROLL = 128
ROUTE_TOKENS = 256
ROUTE_PAUSE = 4
ROUTE_SEGMENTS = (TB_PEER // ROUTE_TOKENS) * 3 * PEER_TOPK // ROUTE_PAUSE
ROWS_PEER = 256
HEADS_PER_STEP = PEER_HEADS * EC_PEER // N_EXPERTS
PACK_BLOCK = 1024
GATE_PITCH = TB_PEER // 2 + SUBLANES


def _rms(x, g):
    return x * lax.rsqrt(jnp.mean(x * x, axis=-1, keepdims=True) + EPS) * g


def _cparams(sem, **flags):
    return pltpu.CompilerParams(dimension_semantics=sem, vmem_limit_bytes=VMEM_LIMIT, flags=flags or None)


def _inproj_kernel(x_ref, g_ref, w_ref, cos_ref, sin_ref, q_ref, k_ref, v_ref, u_ref):
    hn = _rms(x_ref[...], g_ref[...]).astype(BF16)
    proj = jnp.dot(hn, w_ref[...], preferred_element_type=F32)
    cos = cos_ref[...]
    sin = sin_ref[...]
    lane = lax.broadcasted_iota(jnp.int32, cos.shape, 1)
    first_half = (lane % HEAD_DIM) < (HEAD_DIM // 2)

    def rope(z):
        partner = jnp.where(first_half,
                            pltpu.roll(z, LANES - HEAD_DIM // 2, 1),
                            pltpu.roll(z, HEAD_DIM // 2, 1))
        return z * cos + partner * sin

    scale = HEAD_DIM ** -0.5
    for c in range(ATTN_WIDTH // LANES):
        sl = slice(c * LANES, (c + 1) * LANES)
        q_ref[:, sl] = (rope(proj[:, sl]) * scale).astype(BF16)
    k_ref[...] = rope(proj[:, ATTN_WIDTH:ATTN_WIDTH + KV_WIDTH]).astype(BF16)
    v_ref[...] = proj[:, ATTN_WIDTH + KV_WIDTH:ATTN_WIDTH + 2 * KV_WIDTH].astype(BF16)
    c0 = ATTN_WIDTH + 2 * KV_WIDTH
    ga = proj[:, c0:c0 + CONV_CHANNELS]
    gb = proj[:, c0 + CONV_CHANNELS:c0 + 2 * CONV_CHANNELS]
    u_ref[...] = ga * jax.nn.sigmoid(gb)


def _inproj(x2, g, w_bf, cos_t, sin_t, seq):
    n = x2.shape[0]
    tm = TM_PROJ
    sb = seq // tm
    return pl.pallas_call(
        _inproj_kernel,
        grid=(n // tm,),
        in_specs=[
            pl.BlockSpec((tm, D_MODEL), lambda i: (i, 0)),
            pl.BlockSpec((1, D_MODEL), lambda i: (0, 0)),
            pl.BlockSpec((D_MODEL, IN_WIDTH), lambda i: (0, 0)),
            pl.BlockSpec((tm, LANES), lambda i: (i % sb, 0)),
            pl.BlockSpec((tm, LANES), lambda i: (i % sb, 0)),
        ],
        out_specs=[
            pl.BlockSpec((tm, ATTN_WIDTH), lambda i: (i, 0)),
            pl.BlockSpec((tm, KV_WIDTH), lambda i: (i, 0)),
            pl.BlockSpec((tm, KV_WIDTH), lambda i: (i, 0)),
            pl.BlockSpec((tm, CONV_CHANNELS), lambda i: (i, 0)),
        ],
        out_shape=[
            jax.ShapeDtypeStruct((n, ATTN_WIDTH), BF16),
            jax.ShapeDtypeStruct((n, KV_WIDTH), BF16),
            jax.ShapeDtypeStruct((n, KV_WIDTH), BF16),
            jax.ShapeDtypeStruct((n, CONV_CHANNELS), F32),
        ],
        compiler_params=_cparams(("arbitrary",)),
        name="inproj",
    )(x2, g, w_bf, cos_t, sin_t)


def _attn_kernel(sink_ref, q_ref, kp_ref, kc_ref, vp_ref, vc_ref, g_ref, o_ref):
    n = pl.program_id(1)
    kall = jnp.concatenate([kp_ref[...], kc_ref[...]], axis=0)
    vall = jnp.concatenate([vp_ref[...], vc_ref[...]], axis=0)
    qi = lax.broadcasted_iota(jnp.int32, (WINDOW, 2 * WINDOW), 0)
    kj = lax.broadcasted_iota(jnp.int32, (WINDOW, 2 * WINDOW), 1)
    rel = qi + WINDOW - kj
    band = (rel >= 0) & (rel < WINDOW)
    group = ATTN_HEADS // KV_HEADS
    for w in range(SWA_BLOCK // WINDOW):
        rows = slice(w * WINDOW, (w + 1) * WINDOW)
        q = q_ref[rows, :]
        kcat = kall[w * WINDOW:(w + 2) * WINDOW]
        vcat = vall[w * WINDOW:(w + 2) * WINDOW]
        mask = band & ((kj >= WINDOW) | (n > 0)) if w == 0 else band
        outs = []
        for h in range(ATTN_HEADS):
            kv = h // group
            qh = q[:, h * HEAD_DIM:(h + 1) * HEAD_DIM]
            kg = kcat[:, kv * HEAD_DIM:(kv + 1) * HEAD_DIM]
            vg = vcat[:, kv * HEAD_DIM:(kv + 1) * HEAD_DIM]
            s = lax.dot_general(qh, kg, (((1,), (1,)), ((), ())), preferred_element_type=F32)
            s = jnp.where(mask, s, NEG)
            sink = sink_ref[h]
            m = jnp.maximum(jnp.max(s, axis=-1, keepdims=True), sink)
            p = jnp.exp(s - m)
            p = p / (jnp.sum(p, axis=-1, keepdims=True) + jnp.exp(sink - m))
            outs.append(jnp.dot(p.astype(BF16), vg, preferred_element_type=F32))
        attn = jnp.concatenate(outs, axis=1)
        o_ref[rows, :] = _rms(attn, g_ref[...]).astype(BF16)


def _attention(sinks, q, k, v, g, batch, seq):
    n = q.shape[0]
    nb = seq // SWA_BLOCK
    per = SWA_BLOCK // WINDOW
    cur = lambda b, i: (b * nb + i, 0)
    prev = lambda b, i: ((b * nb + i) * per - jnp.minimum(i, 1), 0)
    return pl.pallas_call(
        _attn_kernel,
        grid=(batch, nb),
        in_specs=[
            pl.BlockSpec(memory_space=pltpu.SMEM),
            pl.BlockSpec((SWA_BLOCK, ATTN_WIDTH), cur),
            pl.BlockSpec((WINDOW, KV_WIDTH), prev),
            pl.BlockSpec((SWA_BLOCK, KV_WIDTH), cur),
            pl.BlockSpec((WINDOW, KV_WIDTH), prev),
            pl.BlockSpec((SWA_BLOCK, KV_WIDTH), cur),
            pl.BlockSpec((1, ATTN_WIDTH), lambda b, i: (0, 0)),
        ],
        out_specs=pl.BlockSpec((SWA_BLOCK, ATTN_WIDTH), cur),
        out_shape=jax.ShapeDtypeStruct((n, ATTN_WIDTH), BF16),
        compiler_params=_cparams(("arbitrary", "arbitrary")),
        name="swa",
    )(sinks, q, k, k, v, v, g)


def _conv_kernel(up_ref, uc_ref, w_ref, cb_ref, lg_ref, lb_ref, og_ref, o_ref, sh_ref, y_ref):
    n = pl.program_id(1)
    ts = uc_ref.shape[0]
    rows = CONV_HALO + ts
    tail = up_ref[ts - CONV_HALO:ts, :]
    sh_ref[0, 0:CONV_HALO, :] = jnp.where(n > 0, tail, 0.0)
    sh_ref[0, CONV_HALO:rows, :] = uc_ref[...]
    sh_ref[0, rows:rows + SUBLANES, :] = jnp.zeros((SUBLANES, CONV_CHANNELS), F32)
    for j in range(1, SUBLANES):
        for r0 in range(0, rows, CONV_HALO):
            sh_ref[j, r0:r0 + CONV_HALO, :] = sh_ref[0, r0 + j:r0 + j + CONV_HALO, :]
    off = CONV_HALO - (CONV_KERNEL - 1)

    def conv_rows(r0):
        for c0 in range(0, CONV_CHANNELS, LANES):
            ch = slice(c0, c0 + LANES)
            acc = jnp.zeros((CONV_SUB, LANES), F32)
            for j in range(SUBLANES):
                taps = [k for k in range(CONV_KERNEL) if (off + k) % SUBLANES == j]
                q0 = (off + taps[0]) // SUBLANES
                span = ((off + taps[-1]) // SUBLANES - q0) * SUBLANES + CONV_SUB
                win = sh_ref[j, pl.ds(r0 + q0 * SUBLANES, span), ch]
                for k in taps:
                    o = ((off + k) // SUBLANES - q0) * SUBLANES
                    acc = acc + w_ref[k:k + 1, ch] * win[o:o + CONV_SUB]
            y_ref[pl.ds(r0, CONV_SUB), ch] = acc + cb_ref[:, ch]

    def norm_rows(r0):
        y = y_ref[pl.ds(r0, CONV_SUB), :]
        mu = jnp.mean(y, axis=-1, keepdims=True)
        yc = y - mu
        z = yc * lax.rsqrt(jnp.mean(yc * yc, axis=-1, keepdims=True) + EPS) * lg_ref[...] + lb_ref[...]
        c = z * jax.nn.sigmoid(z)
        o_ref[pl.ds(r0, CONV_SUB), :] = _rms(c, og_ref[...]).astype(BF16)

    def row_block(rb, carry):
        conv_rows(pl.multiple_of(rb * CONV_SUB, CONV_SUB))
        return carry

    lax.fori_loop(0, ts // CONV_SUB, row_block, 0)
    for r0 in range(0, ts, CONV_SUB):
        norm_rows(r0)


def _conv(u, w, cb, lg, lb, og, batch, seq):
    n = u.shape[0]
    ts = TS_CONV
    nb = seq // ts
    cur = lambda b, i: (b * nb + i, 0)
    prev = lambda b, i: (b * nb + jnp.maximum(i - 1, 0), 0)
    vec = pl.BlockSpec((1, CONV_CHANNELS), lambda b, i: (0, 0))
    return pl.pallas_call(
        _conv_kernel,
        grid=(batch, nb),
        in_specs=[
            pl.BlockSpec((ts, CONV_CHANNELS), prev),
            pl.BlockSpec((ts, CONV_CHANNELS), cur),
            pl.BlockSpec((CONV_KERNEL, CONV_CHANNELS), lambda b, i: (0, 0)),
            vec, vec, vec, vec,
        ],
        out_specs=pl.BlockSpec((ts, CONV_CHANNELS), cur),
        out_shape=jax.ShapeDtypeStruct((n, CONV_CHANNELS), BF16),
        scratch_shapes=[pltpu.VMEM((SUBLANES, CONV_HALO + ts + SUBLANES, CONV_CHANNELS), F32),
                        pltpu.VMEM((ts, CONV_CHANNELS), F32)],
        compiler_params=_cparams(("arbitrary", "arbitrary")),
        name="conv",
    )(u, u, w, cb, lg, lb, og)


def _outproj_kernel(x_ref, ma_ref, mc_ref, wo_ref, g_ref, wq_ref, h_ref, hn_ref, pq_ref):
    h = (x_ref[...]
         + jnp.dot(ma_ref[...], wo_ref[0:ATTN_WIDTH, :], preferred_element_type=F32)
         + jnp.dot(mc_ref[...], wo_ref[ATTN_WIDTH:D_MODEL, :], preferred_element_type=F32))
    h_ref[...] = h
    hn = _rms(h, g_ref[...]).astype(BF16)
    hn_ref[...] = hn
    pq_ref[...] = jnp.dot(hn, wq_ref[...], preferred_element_type=F32).astype(BF16)


def _outproj(x2, ma, mc, wo_bf, g, wq_bf):
    n = x2.shape[0]
    tm = TM_PROJ
    qw = PEER_HEADS * PEER_QDIM
    row = lambda i: (i, 0)
    fixed = lambda i: (0, 0)
    return pl.pallas_call(
        _outproj_kernel,
        grid=(n // tm,),
        in_specs=[
            pl.BlockSpec((tm, D_MODEL), row),
            pl.BlockSpec((tm, ATTN_WIDTH), row),
            pl.BlockSpec((tm, CONV_CHANNELS), row),
            pl.BlockSpec((D_MODEL, D_MODEL), fixed),
            pl.BlockSpec((1, D_MODEL), fixed),
            pl.BlockSpec((D_MODEL, qw), fixed),
        ],
        out_specs=[
            pl.BlockSpec((tm, D_MODEL), row),
            pl.BlockSpec((tm, D_MODEL), row),
            pl.BlockSpec((tm, qw), row),
        ],
        out_shape=[
            jax.ShapeDtypeStruct((n, D_MODEL), F32),
            jax.ShapeDtypeStruct((n, D_MODEL), BF16),
            jax.ShapeDtypeStruct((n, qw), BF16),
        ],
        compiler_params=_cparams(("arbitrary",)),
        name="outproj",
    )(x2, ma, mc, wo_bf, g, wq_bf)


def _topk_rows(halves, count):
    big = jnp.int32(2 ** 30)
    (a, ia, pa), (b, ib, pb) = halves
    payload = pa
    a_leads = (a > b) | ((a == b) & (ia < ib))
    lead, rest = jnp.where(a_leads, a, b), jnp.where(a_leads, b, a)
    lead_id, rest_id = jnp.where(a_leads, ia, ib), jnp.where(a_leads, ib, ia)
    if payload is not None:
        lead_pay, rest_pay = jnp.where(a_leads, pa, pb), jnp.where(a_leads, pb, pa)
    vals, ids, pays = [], [], []
    for k in range(count):
        m = jnp.max(lead, axis=0, keepdims=True)
        win = jnp.min(jnp.where(lead == m, lead_id, big), axis=0, keepdims=True)
        hit = lead_id == win
        if payload is not None:
            pays.append(jnp.max(jnp.where(hit, lead_pay, -1), axis=0, keepdims=True))
            lead_pay = jnp.where(hit, rest_pay, lead_pay)
        lead = jnp.where(hit, rest, lead)
        rest = jnp.where(hit, -jnp.inf, rest)
        lead_id = jnp.where(hit, rest_id, lead_id)
        vals.append(m)
        ids.append(win)
        if (k + 1) % ROUTE_PAUSE == 0:
            yield
    return vals, ids, pays


def _route_subkeys(pq_ref, keys_ref, h, p, tok):
    half = PEER_QDIM // 2
    hp = h * 2 + p
    qhp = pq_ref[tok, pl.ds(pl.multiple_of(hp * half, half), half)]
    halves = []
    for k0 in range(0, N_KEYS, N_KEYS // 2):
        s = lax.dot_general(keys_ref[hp, k0:k0 + N_KEYS // 2, :], qhp, (((1,), (1,)), ((), ())),
                            preferred_element_type=F32)
        halves.append((s, lax.broadcasted_iota(jnp.int32, s.shape, 0) + k0, None))
    vals, idx, _ = yield from _topk_rows(halves, PEER_TOPK)
    return vals, idx


def _route_pairs(top_a, top_b, h, ids_scr, gate_scr, tok):
    (va, ia), (vb, ib) = top_a, top_b
    r8 = lax.broadcasted_iota(jnp.int32, (SUBLANES, va[0].shape[1]), 0)
    lo = SUBLANES
    vb_lo = jnp.concatenate(vb[:lo], axis=0)
    ib_lo = jnp.concatenate(ib[:lo], axis=0)
    cand, cid, eid = [], [], []
    for k1 in range(lo):
        ok = (k1 + 1) * (r8 + 1) <= PEER_TOPK
        cand.append(jnp.where(ok, va[k1] + vb_lo, -jnp.inf))
        cid.append(k1 * PEER_TOPK + r8)
        eid.append(ia[k1] * N_KEYS + ib_lo)
    cand.append(va[0] + jnp.concatenate(vb[lo:], axis=0))
    cid.append(lo + r8)
    eid.append(ia[0] * N_KEYS + jnp.concatenate(ib[lo:], axis=0))
    cand.append(jnp.concatenate(va[lo:], axis=0) + vb[0])
    cid.append((lo + r8) * PEER_TOPK)
    eid.append(jnp.concatenate(ia[lo:], axis=0) * N_KEYS + ib[0])
    mid = len(cand) // 2
    halves = [tuple(jnp.concatenate(x[part], axis=0) for x in (cand, cid, eid))
              for part in (slice(0, mid), slice(mid, None))]
    best, _, experts = yield from _topk_rows(halves, PEER_TOPK)
    e = [jnp.exp(b - best[0]) for b in best]
    tot = e[0]
    for x in e[1:]:
        tot = tot + x
    row0 = pl.multiple_of(h * PEER_TOPK, PEER_TOPK)
    gate_scr[pl.ds(row0, PEER_TOPK), tok] = jnp.concatenate([x / tot * 0.5 for x in e], axis=0)
    ids_scr[pl.ds(row0, PEER_TOPK), tok] = jnp.concatenate(experts, axis=0)


def _route_head_steps(pq_ref, keys_ref, h, ids_scr, gate_scr):
    for t0 in range(0, pq_ref.shape[0], ROUTE_TOKENS):
        tok = slice(t0, t0 + ROUTE_TOKENS)
        top_a = yield from _route_subkeys(pq_ref, keys_ref, h, 0, tok)
        top_b = yield from _route_subkeys(pq_ref, keys_ref, h, 1, tok)
        yield from _route_pairs(top_a, top_b, h, ids_scr, gate_scr, tok)


def _route_head(pq_ref, keys_ref, h, ids_scr, gate_scr):
    for _ in _route_head_steps(pq_ref, keys_ref, h, ids_scr, gate_scr):
        pass


def _peer_kernel(pq0_ref, pqn_ref, keys_ref, hn_ref, h_ref, ut_ref, v_ref, fg_ref, o_ref,
                 gs_ref, w_ref, acc_ref, ids_scr, gate_scr, ids_rows, gate_rows, *, final_norm):
    i = pl.program_id(0)
    c = pl.program_id(1)
    tb = hn_ref.shape[0]

    @pl.when((i == 0) & (c == 0))
    def _route_first_block():
        def head(h, carry):
            _route_head(pq0_ref, keys_ref, h, ids_scr, gate_scr)
            return carry
        lax.fori_loop(0, PEER_HEADS, head, 0)

    @pl.when(c == 0)
    def _scatter():
        acc_ref[...] = jnp.zeros_like(acc_ref)
        ids_rows[...] = ids_scr[...].T
        gate_rows[...] = gate_scr[...].T
        nsel = ids_rows.shape[1]
        row_k1 = lax.broadcasted_iota(jnp.int32, (2 * N_KEYS, 2 * nsel), 0)
        row_k2 = lax.broadcasted_iota(jnp.int32, (N_KEYS, 2 * nsel), 0)
        parity = (lax.broadcasted_iota(jnp.int32, (1, 2 * nsel), 1) >= nsel).astype(jnp.int32)

        def token_pairs(it, carry):
            for j in range(SCATTER_UNROLL):
                r = it * SCATTER_UNROLL + j
                ids = jnp.concatenate([ids_rows[pl.ds(2 * r, 1), :], ids_rows[pl.ds(2 * r + 1, 1), :]], axis=1)
                gate = jnp.concatenate([gate_rows[pl.ds(2 * r, 1), :], gate_rows[pl.ds(2 * r + 1, 1), :]], axis=1)
                k1 = lax.shift_right_logical(ids, KEY_BITS) * 2 + parity
                k2 = ids & (N_KEYS - 1)
                m1 = jnp.where(row_k1 == k1, gate, 0.0).astype(BF16)
                m2 = jnp.where(row_k2 == k2, 1.0, 0.0).astype(BF16)
                tile = lax.dot_general(m1, m2, (((1,), (1,)), ((), ())), preferred_element_type=F32)
                gs_ref[pl.ds(r, N_KEYS, stride=GATE_PITCH), :] = pltpu.bitcast(tile.astype(BF16), jnp.uint32)
            return carry

        lax.fori_loop(0, tb // (2 * SCATTER_UNROLL), token_pairs, 0)

    keys_per_sub = ESUB_PEER // N_KEYS

    def expert_weights(sc, r0):
        rows = slice(r0, r0 + ROWS_PEER)
        sl = slice(sc * ESUB_PEER, (sc + 1) * ESUB_PEER)
        ut = pltpu.bitcast(ut_ref[:, sl], BF16)
        act = jnp.dot(hn_ref[rows, :], ut, preferred_element_type=F32)
        k1_0 = c * (EC_PEER // N_KEYS) + sc * keys_per_sub
        gates = jnp.concatenate(
            [pltpu.bitcast(gs_ref[pl.ds(pl.multiple_of((k1_0 + a) * GATE_PITCH + r0 // 2, SUBLANES),
                                        ROWS_PEER // 2), :], BF16) for a in range(keys_per_sub)], axis=1)
        w_ref[rows, sl] = (act * (1.0 + lax.erf(act * (2.0 ** -0.5)))).astype(BF16) * gates

    def expert_outputs(sc, r0):
        rows = slice(r0, r0 + ROWS_PEER)
        v = pltpu.bitcast(v_ref[sc * ESUB_PEER // 2:(sc + 1) * ESUB_PEER // 2, :], BF16)
        acc_ref[rows, :] += jnp.dot(w_ref[rows, sc * ESUB_PEER:(sc + 1) * ESUB_PEER], v,
                                    preferred_element_type=F32)

    mxu_work = []
    for sc in range(EC_PEER // ESUB_PEER):
        for r0 in range(0, tb, ROWS_PEER):
            mxu_work.append(functools.partial(expert_weights, sc, r0))
        if sc > 0:
            for r0 in range(0, tb, ROWS_PEER):
                mxu_work.append(functools.partial(expert_outputs, sc - 1, r0))
    for r0 in range(0, tb, ROWS_PEER):
        mxu_work.append(functools.partial(expert_outputs, EC_PEER // ESUB_PEER - 1, r0))
    routings = [_route_head_steps(pqn_ref, keys_ref, c * HEADS_PER_STEP + j, ids_scr, gate_scr)
                for j in range(HEADS_PER_STEP)]
    issued = 0
    segments = ROUTE_SEGMENTS * len(routings)
    for seg in range(segments):
        while issued * segments < (seg + 1) * len(mxu_work):
            mxu_work[issued]()
            issued += 1
        routing = routings[seg // ROUTE_SEGMENTS]
        next(routing)
        if (seg + 1) % ROUTE_SEGMENTS == 0:
            for _ in routing:
                pass

    @pl.when(c == pl.num_programs(1) - 1)
    def _finish():
        out = h_ref[...] + acc_ref[...]
        o_ref[...] = _rms(out, fg_ref[...]) if final_norm else out


def _peer(pq, keys_bf, hn, h, ut_bf, v_bf, fg, final_norm):
    n = hn.shape[0]
    tb = TB_PEER
    nblk = n // tb
    nsel = PEER_HEADS * PEER_TOPK
    qw = PEER_HEADS * PEER_QDIM
    assert (N_EXPERTS // EC_PEER) * HEADS_PER_STEP == PEER_HEADS
    tok = lambda i, c: (i, 0)
    return pl.pallas_call(
        functools.partial(_peer_kernel, final_norm=final_norm),
        grid=(nblk, N_EXPERTS // EC_PEER),
        in_specs=[
            pl.BlockSpec((tb, qw), lambda i, c: (0, 0), pipeline_mode=pl.Buffered(1)),
            pl.BlockSpec((tb, qw), lambda i, c: (jnp.minimum(i + 1, nblk - 1), 0)),
            pl.BlockSpec((2 * PEER_HEADS, N_KEYS, PEER_QDIM // 2), lambda i, c: (0, 0, 0),
                         pipeline_mode=pl.Buffered(1)),
            pl.BlockSpec((tb, D_MODEL), tok),
            pl.BlockSpec((tb, D_MODEL), tok, pipeline_mode=pl.Buffered(1)),
            pl.BlockSpec((D_MODEL // 2, EC_PEER), lambda i, c: (0, c)),
            pl.BlockSpec((EC_PEER // 2, D_MODEL), lambda i, c: (c, 0)),
            pl.BlockSpec((1, D_MODEL), lambda i, c: (0, 0)),
        ],
        out_specs=pl.BlockSpec((tb, D_MODEL), tok),
        out_shape=jax.ShapeDtypeStruct((n, D_MODEL), F32),
        scratch_shapes=[
            pltpu.VMEM((N_KEYS * GATE_PITCH, N_KEYS), jnp.uint32),
            pltpu.VMEM((tb, EC_PEER), BF16),
            pltpu.VMEM((tb, D_MODEL), F32),
            pltpu.VMEM((nsel, tb), jnp.int32),
            pltpu.VMEM((nsel, tb), F32),
            pltpu.VMEM((tb, nsel), jnp.int32),
            pltpu.VMEM((tb, nsel), F32),
        ],
        compiler_params=_cparams(("arbitrary", "arbitrary")),
        name="peer_experts",
    )(pq, pq, keys_bf, hn, h, ut_bf, v_bf, fg)


def _rope_tables(seq):
    half = HEAD_DIM // 2
    inv = ROPE_THETA ** (-jnp.arange(half, dtype=F32) / half)
    ang = jnp.arange(seq, dtype=jnp.int32).astype(F32)[:, None] * inv[None, :]
    cos = jnp.cos(ang)
    sin = jnp.sin(ang)
    reps = LANES // HEAD_DIM
    cos_t = jnp.tile(jnp.concatenate([cos, cos], axis=1), (1, reps))
    sin_t = jnp.tile(jnp.concatenate([-sin, sin], axis=1), (1, reps))
    return cos_t, sin_t


def _pack_kernel(x_ref, o_ref, *, transpose):
    x = x_ref[...]
    if transpose:
        x = x.T
    o_ref[...] = pltpu.bitcast(x.astype(BF16), jnp.uint32)


def _pack_bf16(w, transpose):
    r, c = w.shape
    if transpose:
        in_spec = pl.BlockSpec((PACK_BLOCK, c), lambda i: (i, 0))
        out_spec = pl.BlockSpec((c // 2, PACK_BLOCK), lambda i: (0, i))
        out_shape = (c // 2, r)
    else:
        in_spec = pl.BlockSpec((PACK_BLOCK, c), lambda i: (i, 0))
        out_spec = pl.BlockSpec((PACK_BLOCK // 2, c), lambda i: (i, 0))
        out_shape = (r // 2, c)
    return pl.pallas_call(
        functools.partial(_pack_kernel, transpose=transpose),
        grid=(r // PACK_BLOCK,),
        in_specs=[in_spec],
        out_specs=out_spec,
        out_shape=jax.ShapeDtypeStruct(out_shape, jnp.uint32),
        compiler_params=_cparams(("arbitrary",)),
        name="pack_t" if transpose else "pack",
    )(w)


def kernel(x, norm_mix_g, w_in, conv_w, conv_b, conv_ln_g, conv_ln_b, attn_sinks, attn_out_g, conv_out_g, w_out,
           norm_ffn_g, peer_w_q, peer_sub_keys, peer_u, peer_v, final_norm_g):
    batch, seq, _ = x.shape
    depth = w_in.shape[0]
    cos_t, sin_t = _rope_tables(seq)
    h = x.reshape(batch * seq, D_MODEL)
    row = lambda a: a.reshape(1, -1)
    for l in range(depth):
        q, k, v, u = _inproj(h, row(norm_mix_g[l]), w_in[l].astype(BF16), cos_t, sin_t, seq)
        ma = _attention(attn_sinks[l], q, k, v, row(attn_out_g[l]), batch, seq)
        mc = _conv(u, conv_w[l], row(conv_b[l]), row(conv_ln_g[l]), row(conv_ln_b[l]), row(conv_out_g[l]),
                   batch, seq)
        h, hn, pq = _outproj(h, ma, mc, w_out[l].astype(BF16), row(norm_ffn_g[l]), peer_w_q[l].astype(BF16))
        keys = peer_sub_keys[l].reshape(2 * PEER_HEADS, N_KEYS, PEER_QDIM // 2).astype(BF16)
        h = _peer(pq, keys, hn, h, _pack_bf16(peer_u[l], transpose=True), _pack_bf16(peer_v[l], transpose=False),
                  row(final_norm_g), final_norm=(l == depth - 1))
    return h.reshape(batch, seq, D_MODEL)
```

```python
import functools

import jax
import jax.numpy as jnp
from jax import lax
from jax.experimental import pallas as pl
from jax.experimental.pallas import tpu as pltpu

F32 = jnp.float32
BF16 = jnp.bfloat16

D_MODEL = 1024
ATTN_HEADS = 8
KV_HEADS = 2
HEAD_DIM = 64
ATTN_WIDTH = ATTN_HEADS * HEAD_DIM
KV_WIDTH = KV_HEADS * HEAD_DIM
CONV_CHANNELS = D_MODEL - ATTN_WIDTH
IN_WIDTH = ATTN_WIDTH + 2 * KV_WIDTH + 2 * CONV_CHANNELS
CONV_KERNEL = 31
WINDOW = 128
ROPE_THETA = 10000.0
PEER_HEADS = 8
N_KEYS = 128
KEY_BITS = N_KEYS.bit_length() - 1
N_EXPERTS = N_KEYS * N_KEYS
PEER_QDIM = 256
PEER_TOPK = 16
EPS = 1e-6
NEG = -1e30

LANES = 128
SUBLANES = 8
VMEM_LIMIT = 60 * 1024 * 1024

TM_PROJ = 1024
SWA_BLOCK = 256
TS_CONV = 512
CONV_HALO = 32
CONV_SUB = 64
TB_PEER = 512
EC_PEER = 2048
ESUB_PEER = 512
SCATTER_UNROLL = 128
ROUTE_TOKENS = 256
ROUTE_PAUSE = 4
ROUTE_SEGMENTS = (TB_PEER // ROUTE_TOKENS) * 3 * PEER_TOPK // ROUTE_PAUSE
ROWS_PEER = 256
HEADS_PER_STEP = PEER_HEADS * EC_PEER // N_EXPERTS
PACK_BLOCK = 1024
GATE_PITCH = TB_PEER // 2 + SUBLANES


def _rms(x, g):
    return x * lax.rsqrt(jnp.mean(x * x, axis=-1, keepdims=True) + EPS) * g


def _cparams(sem, **flags):
    return pltpu.CompilerParams(dimension_semantics=sem, vmem_limit_bytes=VMEM_LIMIT, flags=flags or None)


def _inproj_kernel(x_ref, g_ref, w_ref, cos_ref, sin_ref, q_ref, k_ref, v_ref, u_ref):
    hn = _rms(x_ref[...], g_ref[...]).astype(BF16)
    proj = jnp.dot(hn, w_ref[...], preferred_element_type=F32)
    cos = cos_ref[...]
    sin = sin_ref[...]
    lane = lax.broadcasted_iota(jnp.int32, cos.shape, 1)
    first_half = (lane % HEAD_DIM) < (HEAD_DIM // 2)

    def rope(z):
        partner = jnp.where(first_half,
                            pltpu.roll(z, LANES - HEAD_DIM // 2, 1),
                            pltpu.roll(z, HEAD_DIM // 2, 1))
        return z * cos + partner * sin

    scale = HEAD_DIM ** -0.5
    for c in range(ATTN_WIDTH // LANES):
        sl = slice(c * LANES, (c + 1) * LANES)
        q_ref[:, sl] = (rope(proj[:, sl]) * scale).astype(BF16)
    k_ref[...] = rope(proj[:, ATTN_WIDTH:ATTN_WIDTH + KV_WIDTH]).astype(BF16)
    v_ref[...] = proj[:, ATTN_WIDTH + KV_WIDTH:ATTN_WIDTH + 2 * KV_WIDTH].astype(BF16)
    c0 = ATTN_WIDTH + 2 * KV_WIDTH
    ga = proj[:, c0:c0 + CONV_CHANNELS]
    gb = proj[:, c0 + CONV_CHANNELS:c0 + 2 * CONV_CHANNELS]
    u_ref[...] = ga * jax.nn.sigmoid(gb)


def _inproj(x2, g, w_bf, cos_t, sin_t, seq):
    n = x2.shape[0]
    tm = TM_PROJ
    sb = seq // tm
    return pl.pallas_call(
        _inproj_kernel,
        grid=(n // tm,),
        in_specs=[
            pl.BlockSpec((tm, D_MODEL), lambda i: (i, 0)),
            pl.BlockSpec((1, D_MODEL), lambda i: (0, 0)),
            pl.BlockSpec((D_MODEL, IN_WIDTH), lambda i: (0, 0)),
            pl.BlockSpec((tm, LANES), lambda i: (i % sb, 0)),
            pl.BlockSpec((tm, LANES), lambda i: (i % sb, 0)),
        ],
        out_specs=[
            pl.BlockSpec((tm, ATTN_WIDTH), lambda i: (i, 0)),
            pl.BlockSpec((tm, KV_WIDTH), lambda i: (i, 0)),
            pl.BlockSpec((tm, KV_WIDTH), lambda i: (i, 0)),
            pl.BlockSpec((tm, CONV_CHANNELS), lambda i: (i, 0)),
        ],
        out_shape=[
            jax.ShapeDtypeStruct((n, ATTN_WIDTH), BF16),
            jax.ShapeDtypeStruct((n, KV_WIDTH), BF16),
            jax.ShapeDtypeStruct((n, KV_WIDTH), BF16),
            jax.ShapeDtypeStruct((n, CONV_CHANNELS), F32),
        ],
        compiler_params=_cparams(("arbitrary",)),
        name="inproj",
    )(x2, g, w_bf, cos_t, sin_t)


def _attn_kernel(sink_ref, q_ref, kp_ref, kc_ref, vp_ref, vc_ref, g_ref, o_ref):
    n = pl.program_id(1)
    kall = jnp.concatenate([kp_ref[...], kc_ref[...]], axis=0)
    vall = jnp.concatenate([vp_ref[...], vc_ref[...]], axis=0)
    qi = lax.broadcasted_iota(jnp.int32, (WINDOW, 2 * WINDOW), 0)
    kj = lax.broadcasted_iota(jnp.int32, (WINDOW, 2 * WINDOW), 1)
    rel = qi + WINDOW - kj
    band = (rel >= 0) & (rel < WINDOW)
    group = ATTN_HEADS // KV_HEADS
    for w in range(SWA_BLOCK // WINDOW):
        rows = slice(w * WINDOW, (w + 1) * WINDOW)
        q = q_ref[rows, :]
        kcat = kall[w * WINDOW:(w + 2) * WINDOW]
        vcat = vall[w * WINDOW:(w + 2) * WINDOW]
        mask = band & ((kj >= WINDOW) | (n > 0)) if w == 0 else band
        outs = []
        for h in range(ATTN_HEADS):
            kv = h // group
            qh = q[:, h * HEAD_DIM:(h + 1) * HEAD_DIM]
            kg = kcat[:, kv * HEAD_DIM:(kv + 1) * HEAD_DIM]
            vg = vcat[:, kv * HEAD_DIM:(kv + 1) * HEAD_DIM]
            s = lax.dot_general(qh, kg, (((1,), (1,)), ((), ())), preferred_element_type=F32)
            s = jnp.where(mask, s, NEG)
            sink = sink_ref[h]
            m = jnp.maximum(jnp.max(s, axis=-1, keepdims=True), sink)
            p = jnp.exp(s - m)
            p = p / (jnp.sum(p, axis=-1, keepdims=True) + jnp.exp(sink - m))
            outs.append(jnp.dot(p.astype(BF16), vg, preferred_element_type=F32))
        attn = jnp.concatenate(outs, axis=1)
        o_ref[rows, :] = _rms(attn, g_ref[...]).astype(BF16)


def _attention(sinks, q, k, v, g, batch, seq):
    n = q.shape[0]
    nb = seq // SWA_BLOCK
    per = SWA_BLOCK // WINDOW
    cur = lambda b, i: (b * nb + i, 0)
    prev = lambda b, i: ((b * nb + i) * per - jnp.minimum(i, 1), 0)
    return pl.pallas_call(
        _attn_kernel,
        grid=(batch, nb),
        in_specs=[
            pl.BlockSpec(memory_space=pltpu.SMEM),
            pl.BlockSpec((SWA_BLOCK, ATTN_WIDTH), cur),
            pl.BlockSpec((WINDOW, KV_WIDTH), prev),
            pl.BlockSpec((SWA_BLOCK, KV_WIDTH), cur),
            pl.BlockSpec((WINDOW, KV_WIDTH), prev),
            pl.BlockSpec((SWA_BLOCK, KV_WIDTH), cur),
            pl.BlockSpec((1, ATTN_WIDTH), lambda b, i: (0, 0)),
        ],
        out_specs=pl.BlockSpec((SWA_BLOCK, ATTN_WIDTH), cur),
        out_shape=jax.ShapeDtypeStruct((n, ATTN_WIDTH), BF16),
        compiler_params=_cparams(("arbitrary", "arbitrary")),
        name="swa",
    )(sinks, q, k, k, v, v, g)


def _conv_kernel(up_ref, uc_ref, w_ref, cb_ref, lg_ref, lb_ref, og_ref, o_ref, sh_ref, y_ref):
    n = pl.program_id(1)
    ts = uc_ref.shape[0]
    rows = CONV_HALO + ts
    tail = up_ref[ts - CONV_HALO:ts, :]
    sh_ref[0, 0:CONV_HALO, :] = jnp.where(n > 0, tail, 0.0)
    sh_ref[0, CONV_HALO:rows, :] = uc_ref[...]
    sh_ref[0, rows:rows + SUBLANES, :] = jnp.zeros((SUBLANES, CONV_CHANNELS), F32)
    for j in range(1, SUBLANES):
        for r0 in range(0, rows, CONV_HALO):
            sh_ref[j, r0:r0 + CONV_HALO, :] = sh_ref[0, r0 + j:r0 + j + CONV_HALO, :]
    off = CONV_HALO - (CONV_KERNEL - 1)

    def conv_rows(r0):
        for c0 in range(0, CONV_CHANNELS, LANES):
            ch = slice(c0, c0 + LANES)
            acc = jnp.zeros((CONV_SUB, LANES), F32)
            for j in range(SUBLANES):
                taps = [k for k in range(CONV_KERNEL) if (off + k) % SUBLANES == j]
                q0 = (off + taps[0]) // SUBLANES
                span = ((off + taps[-1]) // SUBLANES - q0) * SUBLANES + CONV_SUB
                win = sh_ref[j, pl.ds(r0 + q0 * SUBLANES, span), ch]
                for k in taps:
                    o = ((off + k) // SUBLANES - q0) * SUBLANES
                    acc = acc + w_ref[k:k + 1, ch] * win[o:o + CONV_SUB]
            y_ref[pl.ds(r0, CONV_SUB), ch] = acc + cb_ref[:, ch]

    def norm_rows(r0):
        y = y_ref[pl.ds(r0, CONV_SUB), :]
        mu = jnp.mean(y, axis=-1, keepdims=True)
        yc = y - mu
        z = yc * lax.rsqrt(jnp.mean(yc * yc, axis=-1, keepdims=True) + EPS) * lg_ref[...] + lb_ref[...]
        c = z * jax.nn.sigmoid(z)
        o_ref[pl.ds(r0, CONV_SUB), :] = _rms(c, og_ref[...]).astype(BF16)

    def row_block(rb, carry):
        conv_rows(pl.multiple_of(rb * CONV_SUB, CONV_SUB))
        return carry

    lax.fori_loop(0, ts // CONV_SUB, row_block, 0)
    for r0 in range(0, ts, CONV_SUB):
        norm_rows(r0)


def _conv(u, w, cb, lg, lb, og, batch, seq):
    n = u.shape[0]
    ts = TS_CONV
    nb = seq // ts
    cur = lambda b, i: (b * nb + i, 0)
    prev = lambda b, i: (b * nb + jnp.maximum(i - 1, 0), 0)
    vec = pl.BlockSpec((1, CONV_CHANNELS), lambda b, i: (0, 0))
    return pl.pallas_call(
        _conv_kernel,
        grid=(batch, nb),
        in_specs=[
            pl.BlockSpec((ts, CONV_CHANNELS), prev),
            pl.BlockSpec((ts, CONV_CHANNELS), cur),
            pl.BlockSpec((CONV_KERNEL, CONV_CHANNELS), lambda b, i: (0, 0)),
            vec, vec, vec, vec,
        ],
        out_specs=pl.BlockSpec((ts, CONV_CHANNELS), cur),
        out_shape=jax.ShapeDtypeStruct((n, CONV_CHANNELS), BF16),
        scratch_shapes=[pltpu.VMEM((SUBLANES, CONV_HALO + ts + SUBLANES, CONV_CHANNELS), F32),
                        pltpu.VMEM((ts, CONV_CHANNELS), F32)],
        compiler_params=_cparams(("arbitrary", "arbitrary")),
        name="conv",
    )(u, u, w, cb, lg, lb, og)


def _outproj_kernel(x_ref, ma_ref, mc_ref, wo_ref, g_ref, wq_ref, h_ref, hn_ref, pq_ref):
    h = (x_ref[...]
         + jnp.dot(ma_ref[...], wo_ref[0:ATTN_WIDTH, :], preferred_element_type=F32)
         + jnp.dot(mc_ref[...], wo_ref[ATTN_WIDTH:D_MODEL, :], preferred_element_type=F32))
    h_ref[...] = h
    hn = _rms(h, g_ref[...]).astype(BF16)
    hn_ref[...] = hn
    pq_ref[...] = jnp.dot(hn, wq_ref[...], preferred_element_type=F32).astype(BF16)


def _outproj(x2, ma, mc, wo_bf, g, wq_bf):
    n = x2.shape[0]
    tm = TM_PROJ
    qw = PEER_HEADS * PEER_QDIM
    row = lambda i: (i, 0)
    fixed = lambda i: (0, 0)
    return pl.pallas_call(
        _outproj_kernel,
        grid=(n // tm,),
        in_specs=[
            pl.BlockSpec((tm, D_MODEL), row),
            pl.BlockSpec((tm, ATTN_WIDTH), row),
            pl.BlockSpec((tm, CONV_CHANNELS), row),
            pl.BlockSpec((D_MODEL, D_MODEL), fixed),
            pl.BlockSpec((1, D_MODEL), fixed),
            pl.BlockSpec((D_MODEL, qw), fixed),
        ],
        out_specs=[
            pl.BlockSpec((tm, D_MODEL), row),
            pl.BlockSpec((tm, D_MODEL), row),
            pl.BlockSpec((tm, qw), row),
        ],
        out_shape=[
            jax.ShapeDtypeStruct((n, D_MODEL), F32),
            jax.ShapeDtypeStruct((n, D_MODEL), BF16),
            jax.ShapeDtypeStruct((n, qw), BF16),
        ],
        compiler_params=_cparams(("arbitrary",)),
        name="outproj",
    )(x2, ma, mc, wo_bf, g, wq_bf)


def _topk_rows(halves, count):
    big = jnp.int32(2 ** 30)
    (a, ia, pa), (b, ib, pb) = halves
    payload = pa
    a_leads = (a > b) | ((a == b) & (ia < ib))
    lead, rest = jnp.where(a_leads, a, b), jnp.where(a_leads, b, a)
    lead_id, rest_id = jnp.where(a_leads, ia, ib), jnp.where(a_leads, ib, ia)
    if payload is not None:
        lead_pay, rest_pay = jnp.where(a_leads, pa, pb), jnp.where(a_leads, pb, pa)
    vals, ids, pays = [], [], []
    for k in range(count):
        m = jnp.max(lead, axis=0, keepdims=True)
        win = jnp.min(jnp.where(lead == m, lead_id, big), axis=0, keepdims=True)
        hit = lead_id == win
        if payload is not None:
            pays.append(jnp.max(jnp.where(hit, lead_pay, -1), axis=0, keepdims=True))
            lead_pay = jnp.where(hit, rest_pay, lead_pay)
        lead = jnp.where(hit, rest, lead)
        rest = jnp.where(hit, -jnp.inf, rest)
        lead_id = jnp.where(hit, rest_id, lead_id)
        vals.append(m)
        ids.append(win)
        if (k + 1) % ROUTE_PAUSE == 0:
            yield
    return vals, ids, pays


def _route_subkeys(pq_ref, keys_ref, h, p, tok):
    half = PEER_QDIM // 2
    hp = h * 2 + p
    qhp = pq_ref[tok, pl.ds(pl.multiple_of(hp * half, half), half)]
    halves = []
    for k0 in range(0, N_KEYS, N_KEYS // 2):
        s = lax.dot_general(keys_ref[hp, k0:k0 + N_KEYS // 2, :], qhp, (((1,), (1,)), ((), ())),
                            preferred_element_type=F32)
        halves.append((s, lax.broadcasted_iota(jnp.int32, s.shape, 0) + k0, None))
    vals, idx, _ = yield from _topk_rows(halves, PEER_TOPK)
    return vals, idx


def _route_pairs(top_a, top_b, h, ids_scr, gate_scr, tok):
    (va, ia), (vb, ib) = top_a, top_b
    r8 = lax.broadcasted_iota(jnp.int32, (SUBLANES, va[0].shape[1]), 0)
    lo = SUBLANES
    vb_lo = jnp.concatenate(vb[:lo], axis=0)
    ib_lo = jnp.concatenate(ib[:lo], axis=0)
    cand, cid, eid = [], [], []
    for k1 in range(lo):
        ok = (k1 + 1) * (r8 + 1) <= PEER_TOPK
        cand.append(jnp.where(ok, va[k1] + vb_lo, -jnp.inf))
        cid.append(k1 * PEER_TOPK + r8)
        eid.append(ia[k1] * N_KEYS + ib_lo)
    cand.append(va[0] + jnp.concatenate(vb[lo:], axis=0))
    cid.append(lo + r8)
    eid.append(ia[0] * N_KEYS + jnp.concatenate(ib[lo:], axis=0))
    cand.append(jnp.concatenate(va[lo:], axis=0) + vb[0])
    cid.append((lo + r8) * PEER_TOPK)
    eid.append(jnp.concatenate(ia[lo:], axis=0) * N_KEYS + ib[0])
    mid = len(cand) // 2
    halves = [tuple(jnp.concatenate(x[part], axis=0) for x in (cand, cid, eid))
              for part in (slice(0, mid), slice(mid, None))]
    best, _, experts = yield from _topk_rows(halves, PEER_TOPK)
    e = [jnp.exp(b - best[0]) for b in best]
    tot = e[0]
    for x in e[1:]:
        tot = tot + x
    row0 = pl.multiple_of(h * PEER_TOPK, PEER_TOPK)
    gate_scr[pl.ds(row0, PEER_TOPK), tok] = jnp.concatenate([x / tot * 0.5 for x in e], axis=0)
    ids_scr[pl.ds(row0, PEER_TOPK), tok] = jnp.concatenate(experts, axis=0)


def _route_head_steps(pq_ref, keys_ref, h, ids_scr, gate_scr):
    for t0 in range(0, pq_ref.shape[0], ROUTE_TOKENS):
        tok = slice(t0, t0 + ROUTE_TOKENS)
        top_a = yield from _route_subkeys(pq_ref, keys_ref, h, 0, tok)
        top_b = yield from _route_subkeys(pq_ref, keys_ref, h, 1, tok)
        yield from _route_pairs(top_a, top_b, h, ids_scr, gate_scr, tok)


def _route_head(pq_ref, keys_ref, h, ids_scr, gate_scr):
    for _ in _route_head_steps(pq_ref, keys_ref, h, ids_scr, gate_scr):
        pass


def _peer_kernel(pq0_ref, pqn_ref, keys_ref, hn_ref, h_ref, ut_ref, v_ref, fg_ref, o_ref,
                 gs_ref, w_ref, acc_ref, ids_scr, gate_scr, ids_rows, gate_rows, *, final_norm):
    i = pl.program_id(0)
    c = pl.program_id(1)
    tb = hn_ref.shape[0]

    @pl.when((i == 0) & (c == 0))
    def _route_first_block():
        def head(h, carry):
            _route_head(pq0_ref, keys_ref, h, ids_scr, gate_scr)
            return carry
        lax.fori_loop(0, PEER_HEADS, head, 0)

    @pl.when(c == 0)
    def _scatter():
        acc_ref[...] = jnp.zeros_like(acc_ref)
        ids_rows[...] = ids_scr[...].T
        gate_rows[...] = gate_scr[...].T
        nsel = ids_rows.shape[1]
        row_k1 = lax.broadcasted_iota(jnp.int32, (2 * N_KEYS, 2 * nsel), 0)
        row_k2 = lax.broadcasted_iota(jnp.int32, (N_KEYS, 2 * nsel), 0)
        parity = (lax.broadcasted_iota(jnp.int32, (1, 2 * nsel), 1) >= nsel).astype(jnp.int32)

        def token_pairs(it, carry):
            for j in range(SCATTER_UNROLL):
                r = it * SCATTER_UNROLL + j
                ids = jnp.concatenate([ids_rows[pl.ds(2 * r, 1), :], ids_rows[pl.ds(2 * r + 1, 1), :]], axis=1)
                gate = jnp.concatenate([gate_rows[pl.ds(2 * r, 1), :], gate_rows[pl.ds(2 * r + 1, 1), :]], axis=1)
                k1 = lax.shift_right_logical(ids, KEY_BITS) * 2 + parity
                k2 = ids & (N_KEYS - 1)
                m1 = jnp.where(row_k1 == k1, gate, 0.0).astype(BF16)
                m2 = jnp.where(row_k2 == k2, 1.0, 0.0).astype(BF16)
                tile = lax.dot_general(m1, m2, (((1,), (1,)), ((), ())), preferred_element_type=F32)
                gs_ref[pl.ds(r, N_KEYS, stride=GATE_PITCH), :] = pltpu.bitcast(tile.astype(BF16), jnp.uint32)
            return carry

        lax.fori_loop(0, tb // (2 * SCATTER_UNROLL), token_pairs, 0)

    keys_per_sub = ESUB_PEER // N_KEYS

    def expert_weights(sc, r0):
        rows = slice(r0, r0 + ROWS_PEER)
        sl = slice(sc * ESUB_PEER, (sc + 1) * ESUB_PEER)
        ut = pltpu.bitcast(ut_ref[:, sl], BF16)
        act = jnp.dot(hn_ref[rows, :], ut, preferred_element_type=F32)
        k1_0 = c * (EC_PEER // N_KEYS) + sc * keys_per_sub
        gates = jnp.concatenate(
            [pltpu.bitcast(gs_ref[pl.ds(pl.multiple_of((k1_0 + a) * GATE_PITCH + r0 // 2, SUBLANES),
                                        ROWS_PEER // 2), :], BF16) for a in range(keys_per_sub)], axis=1)
        w_ref[rows, sl] = (act * (1.0 + lax.erf(act * (2.0 ** -0.5)))).astype(BF16) * gates

    def expert_outputs(sc, r0):
        rows = slice(r0, r0 + ROWS_PEER)
        v = pltpu.bitcast(v_ref[sc * ESUB_PEER // 2:(sc + 1) * ESUB_PEER // 2, :], BF16)
        acc_ref[rows, :] += jnp.dot(w_ref[rows, sc * ESUB_PEER:(sc + 1) * ESUB_PEER], v,
                                    preferred_element_type=F32)

    mxu_work = []
    for sc in range(EC_PEER // ESUB_PEER):
        for r0 in range(0, tb, ROWS_PEER):
            mxu_work.append(functools.partial(expert_weights, sc, r0))
        if sc > 0:
            for r0 in range(0, tb, ROWS_PEER):
                mxu_work.append(functools.partial(expert_outputs, sc - 1, r0))
    for r0 in range(0, tb, ROWS_PEER):
        mxu_work.append(functools.partial(expert_outputs, EC_PEER // ESUB_PEER - 1, r0))
    routings = [_route_head_steps(pqn_ref, keys_ref, c * HEADS_PER_STEP + j, ids_scr, gate_scr)
                for j in range(HEADS_PER_STEP)]
    issued = 0
    segments = ROUTE_SEGMENTS * len(routings)
    for seg in range(segments):
        while issued * segments < (seg + 1) * len(mxu_work):
            mxu_work[issued]()
            issued += 1
        routing = routings[seg // ROUTE_SEGMENTS]
        next(routing)
        if (seg + 1) % ROUTE_SEGMENTS == 0:
            for _ in routing:
                pass

    @pl.when(c == pl.num_programs(1) - 1)
    def _finish():
        out = h_ref[...] + acc_ref[...]
        o_ref[...] = _rms(out, fg_ref[...]) if final_norm else out


def _peer(pq, keys_bf, hn, h, ut_bf, v_bf, fg, final_norm):
    n = hn.shape[0]
    tb = TB_PEER
    nblk = n // tb
    nsel = PEER_HEADS * PEER_TOPK
    qw = PEER_HEADS * PEER_QDIM
    assert (N_EXPERTS // EC_PEER) * HEADS_PER_STEP == PEER_HEADS
    tok = lambda i, c: (i, 0)
    return pl.pallas_call(
        functools.partial(_peer_kernel, final_norm=final_norm),
        grid=(nblk, N_EXPERTS // EC_PEER),
        in_specs=[
            pl.BlockSpec((tb, qw), lambda i, c: (0, 0), pipeline_mode=pl.Buffered(1)),
            pl.BlockSpec((tb, qw), lambda i, c: (jnp.minimum(i + 1, nblk - 1), 0)),
            pl.BlockSpec((2 * PEER_HEADS, N_KEYS, PEER_QDIM // 2), lambda i, c: (0, 0, 0),
                         pipeline_mode=pl.Buffered(1)),
            pl.BlockSpec((tb, D_MODEL), tok),
            pl.BlockSpec((tb, D_MODEL), tok, pipeline_mode=pl.Buffered(1)),
            pl.BlockSpec((D_MODEL // 2, EC_PEER), lambda i, c: (0, c)),
            pl.BlockSpec((EC_PEER // 2, D_MODEL), lambda i, c: (c, 0)),
            pl.BlockSpec((1, D_MODEL), lambda i, c: (0, 0)),
        ],
        out_specs=pl.BlockSpec((tb, D_MODEL), tok),
        out_shape=jax.ShapeDtypeStruct((n, D_MODEL), F32),
        scratch_shapes=[
            pltpu.VMEM((N_KEYS * GATE_PITCH, N_KEYS), jnp.uint32),
            pltpu.VMEM((tb, EC_PEER), BF16),
            pltpu.VMEM((tb, D_MODEL), F32),
            pltpu.VMEM((nsel, tb), jnp.int32),
            pltpu.VMEM((nsel, tb), F32),
            pltpu.VMEM((tb, nsel), jnp.int32),
            pltpu.VMEM((tb, nsel), F32),
        ],
        compiler_params=_cparams(("arbitrary", "arbitrary")),
        name="peer_experts",
    )(pq, pq, keys_bf, hn, h, ut_bf, v_bf, fg)


def _rope_tables(seq):
    half = HEAD_DIM // 2
    inv = ROPE_THETA ** (-jnp.arange(half, dtype=F32) / half)
    ang = jnp.arange(seq, dtype=jnp.int32).astype(F32)[:, None] * inv[None, :]
    cos = jnp.cos(ang)
    sin = jnp.sin(ang)
    reps = LANES // HEAD_DIM
    cos_t = jnp.tile(jnp.concatenate([cos, cos], axis=1), (1, reps))
    sin_t = jnp.tile(jnp.concatenate([-sin, sin], axis=1), (1, reps))
    return cos_t, sin_t


def _pack_kernel(u_ref, v_ref, ut_ref, vp_ref):
    ut_ref[...] = pltpu.bitcast(u_ref[...].T.astype(BF16), jnp.uint32)
    vp_ref[...] = pltpu.bitcast(v_ref[...].astype(BF16), jnp.uint32)


def _pack_tables(u, v):
    e, d = u.shape
    return pl.pallas_call(
        _pack_kernel,
        grid=(e // PACK_BLOCK,),
        in_specs=[pl.BlockSpec((PACK_BLOCK, d), lambda i: (i, 0)), pl.BlockSpec((PACK_BLOCK, d), lambda i: (i, 0))],
        out_specs=[pl.BlockSpec((d // 2, PACK_BLOCK), lambda i: (0, i)),
                   pl.BlockSpec((PACK_BLOCK // 2, d), lambda i: (i, 0))],
        out_shape=[jax.ShapeDtypeStruct((d // 2, e), jnp.uint32), jax.ShapeDtypeStruct((e // 2, d), jnp.uint32)],
        compiler_params=_cparams(("arbitrary",)),
        name="pack",
    )(u, v)


def kernel(x, norm_mix_g, w_in, conv_w, conv_b, conv_ln_g, conv_ln_b, attn_sinks, attn_out_g, conv_out_g, w_out,
           norm_ffn_g, peer_w_q, peer_sub_keys, peer_u, peer_v, final_norm_g):
    batch, seq, _ = x.shape
    depth = w_in.shape[0]
    cos_t, sin_t = _rope_tables(seq)
    h = x.reshape(batch * seq, D_MODEL)
    row = lambda a: a.reshape(1, -1)
    for l in range(depth):
        q, k, v, u = _inproj(h, row(norm_mix_g[l]), w_in[l].astype(BF16), cos_t, sin_t, seq)
        ma = _attention(attn_sinks[l], q, k, v, row(attn_out_g[l]), batch, seq)
        mc = _conv(u, conv_w[l], row(conv_b[l]), row(conv_ln_g[l]), row(conv_ln_b[l]), row(conv_out_g[l]),
                   batch, seq)
        h, hn, pq = _outproj(h, ma, mc, w_out[l].astype(BF16), row(norm_ffn_g[l]), peer_w_q[l].astype(BF16))
        keys = peer_sub_keys[l].reshape(2 * PEER_HEADS, N_KEYS, PEER_QDIM // 2).astype(BF16)
        ut_packed, v_packed = _pack_tables(peer_u[l], peer_v[l])
        h = _peer(pq, keys, hn, h, ut_packed, v_packed,
                  row(final_norm_g), final_norm=(l == depth - 1))
    return h.reshape(batch, seq, D_MODEL)
```
